```python
import jax, jax.numpy as jnp
from jax import lax
import numpy as np

D_MODEL = 1024
BATCH = 1
SEQ = 16384
DEPTH = 4
DEC_BATCH = 8
DEC_SEQ = 8192
PAST_LEN = 128

GLA_HEADS = 4
GLA_DK = D_MODEL // 2 // GLA_HEADS
GLA_DV = D_MODEL // GLA_HEADS
GLA_LOWRANK = 16
GLA_TAU = 16.0
GLA_CHUNK = 64
QK_W = GLA_HEADS * GLA_DK
V_W = GLA_HEADS * GLA_DV
POOL_GROUPS = 4
POOL_WIDTH = D_MODEL // 2
POOL_GC = POOL_WIDTH // POOL_GROUPS
POOL_WINDOWS = (2, 4, 8, 16)
D_FF = 2816
CONV_W = 3
EPS = 1e-6

IN_WIDTHS = (QK_W, QK_W, V_W, V_W, GLA_LOWRANK, GLA_LOWRANK, POOL_WIDTH, D_MODEL, D_MODEL)
IN_WIDTH = int(sum(IN_WIDTHS))
IN_SPLITS = tuple(int(c) for c in np.cumsum(IN_WIDTHS)[:-1])

kernel_name = "hybrid_gla_pool_convffn_encoder"


def rmsnorm(x, g):
    x32 = x.astype(jnp.float32)
    y = x32 * lax.rsqrt(jnp.mean(x32 * x32, axis=-1, keepdims=True) + EPS) * g.astype(jnp.float32)
    return y.astype(x.dtype)


def gla_causal(q, k, v, g):
    B, H, S, DK = q.shape
    DV = v.shape[-1]
    n = S // GLA_CHUNK

    def to_chunks(t):
        return jnp.moveaxis(t.reshape(B, H, n, GLA_CHUNK, t.shape[-1]), 2, 0)

    qc, kc, vc, gc = to_chunks(q), to_chunks(k), to_chunks(v), to_chunks(g)
    mask = jnp.tril(jnp.ones((GLA_CHUNK, GLA_CHUNK), dtype=bool))[:, :, None]

    def step(state, inp):
        qi, ki, vi, gi = inp
        b = jnp.cumsum(gi, axis=2)
        o_inter = jnp.einsum('bhik,bhkv->bhiv', qi * jnp.exp(b), state)
        diff = b[:, :, :, None, :] - b[:, :, None, :, :]
        decay = jnp.exp(jnp.where(mask, diff, -jnp.inf))
        scores = jnp.einsum('bhik,bhijk,bhjk->bhij', qi, decay, ki)
        o_intra = jnp.einsum('bhij,bhjv->bhiv', scores, vi)
        b_last = b[:, :, -1:, :]
        k_dec = ki * jnp.exp(b_last - b)
        state = jnp.exp(b_last[:, :, 0, :])[..., None] * state + jnp.einsum('bhjk,bhjv->bhkv', k_dec, vi)
        return state, o_inter + o_intra

    s0 = jnp.zeros((B, H, DK, DV), jnp.float32)
    _, o = lax.scan(step, s0, (qc, kc, vc, gc))
    return jnp.moveaxis(o, 0, 2).reshape(B, H, S, DV)


def pool_branch(u, w_grp, scale):
    B, S, _ = u.shape
    ug = u.reshape(B, S, POOL_GROUPS, POOL_GC).astype(jnp.float32)
    cs = jnp.concatenate([jnp.zeros((B, 1, POOL_GROUPS, POOL_GC), jnp.float32), jnp.cumsum(ug, axis=1)], axis=1)
    pos = jnp.arange(S)
    outs = []
    for gi, w in enumerate(POOL_WINDOWS):
        lo = jnp.clip(pos - w // 2, 0, S - 1)
        hi = jnp.clip(pos + (w - 1 - w // 2), 0, S - 1)
        csg = cs[:, :, gi]
        wsum = jnp.take(csg, hi + 1, axis=1) - jnp.take(csg, lo, axis=1)
        cnt = (hi - lo + 1).astype(jnp.float32)[None, :, None]
        outs.append(wsum / cnt - ug[:, :, gi])
    p = jnp.stack(outs, axis=2)
    p = jnp.einsum('bsgc,gcd->bsgd', p, w_grp.astype(jnp.float32)).reshape(B, S, POOL_WIDTH)
    return (p * scale.astype(jnp.float32)).astype(u.dtype)


def mixer(xn, w_in, w_lr2_f, b_lr_f, w_lr2_b, b_lr_b, onorm_g, w_pool_grp, pool_scale, w_br_a, w_br_b, w_out):
    B, S, _ = xn.shape
    h = xn @ w_in
    q, k, v, r, lr_f, lr_b, u, ga, gb = jnp.split(h, IN_SPLITS, axis=-1)

    def heads(t):
        return t.reshape(B, S, GLA_HEADS, -1).transpose(0, 2, 1, 3).astype(jnp.float32)

    qh = heads(q) * (GLA_DK ** -0.5)
    kh = heads(k)
    vh = heads(v)
    g_f = heads(jax.nn.log_sigmoid((lr_f @ w_lr2_f + b_lr_f).astype(jnp.float32)) / GLA_TAU)
    g_b = heads(jax.nn.log_sigmoid((lr_b @ w_lr2_b + b_lr_b).astype(jnp.float32)) / GLA_TAU)

    def flip(t):
        return jnp.flip(t, axis=2)

    o = gla_causal(qh, kh, vh, g_f) + flip(gla_causal(flip(qh), flip(kh), flip(vh), flip(g_b)))
    o = o * lax.rsqrt(jnp.mean(o * o, axis=-1, keepdims=True) + EPS)
    o = o.transpose(0, 2, 1, 3).reshape(B, S, V_W) * onorm_g.astype(jnp.float32)
    a_out = (o.astype(xn.dtype) * jax.nn.silu(r)) @ w_br_a

    b_out = pool_branch(u, w_pool_grp, pool_scale) @ w_br_b

    merged = jax.nn.sigmoid(ga) * a_out + jax.nn.sigmoid(gb) * b_out
    return merged @ w_out


def conv_ffn(xn, w_up, conv_w, conv_b, w_down):
    h = xn @ w_up
    a, val = jnp.split(h, [D_FF], axis=-1)
    ap = jnp.pad(a, ((0, 0), (1, 1), (0, 0)))
    a = ap[:, :-2] * conv_w[0] + ap[:, 1:-1] * conv_w[1] + ap[:, 2:] * conv_w[2] + conv_b
    return (jax.nn.silu(a) * val) @ w_down


def trunk(x, norm1_g, w_in, w_lr2_f, b_lr_f, w_lr2_b, b_lr_b, onorm_g, w_pool_grp, pool_scale,
          w_br_a, w_br_b, w_out, norm2_g, w_up, conv_w, conv_b, w_down, final_g):
    for l in range(DEPTH):
        xn = rmsnorm(x, norm1_g[l])
        x = x + mixer(xn, w_in[l], w_lr2_f[l], b_lr_f[l], w_lr2_b[l], b_lr_b[l], onorm_g[l],
                      w_pool_grp[l], pool_scale[l], w_br_a[l], w_br_b[l], w_out[l])
        xn = rmsnorm(x, norm2_g[l])
        x = x + conv_ffn(xn, w_up[l], conv_w[l], conv_b[l], w_down[l])
    return rmsnorm(x, final_g)


def setup_inputs(seed: int = 0) -> dict:
    key = jax.random.key(seed)
    ks = jax.random.split(key, 24)

    def nrm(k, shape, scale):
        return jax.random.normal(k, shape, jnp.float32) * scale

    L = DEPTH
    return {
        "x_prompt": nrm(ks[0], (BATCH, SEQ, D_MODEL), 1.0),
        "x_sample": nrm(ks[1], (DEC_BATCH, DEC_SEQ, D_MODEL), 1.0),
        "norm1_g": 1.0 + nrm(ks[2], (L, D_MODEL), 0.02),
        "w_in": nrm(ks[3], (L, D_MODEL, IN_WIDTH), D_MODEL ** -0.5),
        "w_lr2_f": nrm(ks[4], (L, GLA_LOWRANK, QK_W), GLA_LOWRANK ** -0.5),
        "b_lr_f": nrm(ks[5], (L, QK_W), 0.1),
        "w_lr2_b": nrm(ks[6], (L, GLA_LOWRANK, QK_W), GLA_LOWRANK ** -0.5),
        "b_lr_b": nrm(ks[7], (L, QK_W), 0.1),
        "onorm_g": 1.0 + nrm(ks[8], (L, V_W), 0.02),
        "w_pool_grp": nrm(ks[9], (L, POOL_GROUPS, POOL_GC, POOL_GC), POOL_GC ** -0.5),
        "pool_scale": 1.0 + nrm(ks[10], (L, POOL_WIDTH), 0.02),
        "w_br_a": nrm(ks[11], (L, V_W, D_MODEL), V_W ** -0.5),
        "w_br_b": nrm(ks[12], (L, POOL_WIDTH, D_MODEL), POOL_WIDTH ** -0.5),
        "w_out": nrm(ks[13], (L, D_MODEL, D_MODEL), D_MODEL ** -0.5),
        "norm2_g": 1.0 + nrm(ks[14], (L, D_MODEL), 0.02),
        "w_up": nrm(ks[15], (L, D_MODEL, 2 * D_FF), D_MODEL ** -0.5),
        "conv_w": nrm(ks[16], (L, CONV_W, D_FF), CONV_W ** -0.5),
        "conv_b": nrm(ks[17], (L, D_FF), 0.02),
        "w_down": nrm(ks[18], (L, D_FF, D_MODEL), D_FF ** -0.5),
        "final_g": 1.0 + nrm(ks[19], (D_MODEL,), 0.02),
    }


def reference(x_prompt, x_sample, norm1_g, w_in, w_lr2_f, b_lr_f, w_lr2_b, b_lr_b, onorm_g, w_pool_grp,
              pool_scale, w_br_a, w_br_b, w_out, norm2_g, w_up, conv_w, conv_b, w_down, final_g):
    y_prompt = trunk(x_prompt, norm1_g, w_in, w_lr2_f, b_lr_f, w_lr2_b, b_lr_b, onorm_g, w_pool_grp, pool_scale,
                     w_br_a, w_br_b, w_out, norm2_g, w_up, conv_w, conv_b, w_down, final_g)
    y_sample = trunk(x_sample, norm1_g, w_in, w_lr2_f, b_lr_f, w_lr2_b, b_lr_b, onorm_g, w_pool_grp, pool_scale,
                     w_br_a, w_br_b, w_out, norm2_g, w_up, conv_w, conv_b, w_down, final_g)
    return (y_prompt, y_sample)
```

```python
import functools

import jax
import jax.numpy as jnp
from jax import lax
from jax.experimental import pallas as pl
from jax.experimental.pallas import tpu as pltpu

D_MODEL = 1024
GLA_HEADS = 4
GLA_DK = 128
GLA_DV = 256
QK_W = GLA_HEADS * GLA_DK
V_W = GLA_HEADS * GLA_DV
GLA_LOWRANK = 16
GLA_TAU = 16.0
GLA_CHUNK = 64
POOL_GROUPS = 4
POOL_WIDTH = 512
POOL_GC = 128
POOL_WINDOWS = (2, 4, 8, 16)
D_FF = 2816
EPS = 1e-6

V7X_VMEM_BYTES = 64 * 1024 * 1024
V7X_LANES = 128
BF16_SUBLANE_TILE = 16

LR_PAD = V7X_LANES
HALO = BF16_SUBLANE_TILE
MAIN_W = 2 * QK_W + 2 * V_W + POOL_WIDTH + 2 * D_MODEL

F32 = jnp.float32
BF16 = jnp.bfloat16


def _vmem_limit(block_bytes, scratch_bytes):
    want = 2 * block_bytes + scratch_bytes + 16 * 1024 * 1024
    return int(min(want, V7X_VMEM_BYTES - 8 * 1024 * 1024))


def _nbytes(shape, dtype):
    n = 1
    for s in shape:
        n *= s
    return n * jnp.dtype(dtype).itemsize


def _rmsnorm(x, g):
    return x * lax.rsqrt(jnp.mean(x * x, axis=-1, keepdims=True) + EPS) * g


def _dot(a, b):
    return jnp.dot(a, b, preferred_element_type=F32)


def _dot_nt(a, b):
    return lax.dot_general(a, b, (((1,), (1,)), ((), ())), preferred_element_type=F32)


def _dot_tn(a, b):
    return lax.dot_general(a, b, (((0,), (0,)), ((), ())), preferred_element_type=F32)


def _in_proj_kernel(x_ref, g1_ref, wm_ref, wlr_ref, wlr2_ref, blr_ref,
                    q_ref, k_ref, v_ref, sr_ref, u_ref, sga_ref, sgb_ref, g_ref):
    xn = _rmsnorm(x_ref[...], g1_ref[...]).astype(BF16)

    def proj(c0, w):
        return _dot(xn, wm_ref[:, c0:c0 + w])

    cw = 512
    q_ref[...] = (proj(0, QK_W) * (GLA_DK ** -0.5)).astype(BF16)
    k_ref[...] = proj(QK_W, QK_W).astype(BF16)
    c = 2 * QK_W
    for j in range(V_W // cw):
        v_ref[:, j * cw:(j + 1) * cw] = proj(c + j * cw, cw).astype(BF16)
    c += V_W
    for j in range(V_W // cw):
        r = proj(c + j * cw, cw)
        sr_ref[:, j * cw:(j + 1) * cw] = (r * jax.nn.sigmoid(r)).astype(BF16)
    c += V_W
    u_ref[...] = proj(c, POOL_WIDTH).astype(BF16)
    c += POOL_WIDTH
    for j in range(D_MODEL // cw):
        sga_ref[:, j * cw:(j + 1) * cw] = jax.nn.sigmoid(proj(c + j * cw, cw)).astype(BF16)
    c += D_MODEL
    for j in range(D_MODEL // cw):
        sgb_ref[:, j * cw:(j + 1) * cw] = jax.nn.sigmoid(proj(c + j * cw, cw)).astype(BF16)

    lr = _dot(xn, wlr_ref[...]).astype(BF16)
    z = _dot(lr, wlr2_ref[...]) + blr_ref[...]
    g_ref[...] = (jnp.minimum(z, 0.0) - jnp.log1p(jnp.exp(-jnp.abs(z)))) * (1.0 / GLA_TAU)


def _in_proj(x, g1, wm, wlr, wlr2, blr, tile):
    n = x.shape[0]
    grid = (n // tile,)
    row = lambda i: (i, 0)
    full = lambda i: (0, 0)
    outs = [(QK_W, BF16), (QK_W, BF16), (V_W, BF16), (V_W, BF16), (POOL_WIDTH, BF16),
            (D_MODEL, BF16), (D_MODEL, BF16), (2 * QK_W, F32)]
    in_specs = [
        pl.BlockSpec((tile, D_MODEL), row),
        pl.BlockSpec((1, D_MODEL), full),
        pl.BlockSpec((D_MODEL, MAIN_W), full),
        pl.BlockSpec((D_MODEL, LR_PAD), full),
        pl.BlockSpec((LR_PAD, 2 * QK_W), full),
        pl.BlockSpec((1, 2 * QK_W), full),
    ]
    block_bytes = _nbytes((tile, D_MODEL), F32) + sum(_nbytes((tile, w), d) for w, d in outs)
    weight_bytes = _nbytes((D_MODEL, MAIN_W + LR_PAD), BF16) + _nbytes((LR_PAD, 2 * QK_W), BF16)
    return pl.pallas_call(
        _in_proj_kernel,
        grid=grid,
        in_specs=in_specs,
        out_specs=[pl.BlockSpec((tile, w), row) for w, _ in outs],
        out_shape=[jax.ShapeDtypeStruct((n, w), d) for w, d in outs],
        compiler_params=pltpu.CompilerParams(
            dimension_semantics=("arbitrary",),
            vmem_limit_bytes=_vmem_limit(block_bytes, weight_bytes)),
        name="in_proj",
    )(x, g1, wm, wlr, wlr2, blr)


def _gla_chunk(r0, q_ref, k_ref, v_ref, g_ref, o_ref, s_ref, cum, mask, edge_row):
    rows = pl.ds(r0, GLA_CHUNK)
    g = g_ref[rows, :]
    b = jnp.dot(cum, g, precision=lax.Precision.HIGHEST, preferred_element_type=F32)
    q = q_ref[rows, :].astype(F32)
    k = k_ref[rows, :].astype(F32)
    v = v_ref[rows, :]
    b_edge = b[edge_row:edge_row + 1, :]
    q_dec = (q * jnp.exp(b)).astype(BF16)
    k_inv = (k * jnp.exp(-b)).astype(BF16)
    k_dec = (k * jnp.exp(b_edge - b)).astype(BF16)
    s_decay = jnp.exp(b_edge)
    for h in range(GLA_HEADS):
        ks = slice(h * GLA_DK, (h + 1) * GLA_DK)
        vs = slice(h * GLA_DV, (h + 1) * GLA_DV)
        s = s_ref[h]
        o_inter = _dot_nt(q_dec[:, ks], s.astype(BF16))
        scores = _dot_nt(q_dec[:, ks], k_inv[:, ks])
        scores = jnp.where(mask, scores, 0.0).astype(BF16)
        o = o_inter + _dot(scores, v[:, vs])
        o_ref[rows, vs] = o.astype(o_ref.dtype)
        s_ref[h] = s * s_decay[:, ks] + _dot_tn(v[:, vs], k_dec[:, ks])


def _gla_kernel(qf_ref, kf_ref, vf_ref, gf_ref, qb_ref, kb_ref, vb_ref, gb_ref,
                of_ref, ob_ref, sf_ref, sb_ref, *, n_chunks):
    @pl.when(pl.program_id(1) == 0)
    def _():
        sf_ref[...] = jnp.zeros_like(sf_ref)
        sb_ref[...] = jnp.zeros_like(sb_ref)

    ri = lax.broadcasted_iota(jnp.int32, (GLA_CHUNK, GLA_CHUNK), 0)
    ci = lax.broadcasted_iota(jnp.int32, (GLA_CHUNK, GLA_CHUNK), 1)
    lower = ri >= ci
    upper = ri <= ci
    cum_f = lower.astype(F32)
    cum_b = upper.astype(F32)

    def body(c, carry):
        rf = pl.multiple_of(c * GLA_CHUNK, GLA_CHUNK)
        rb = pl.multiple_of((n_chunks - 1 - c) * GLA_CHUNK, GLA_CHUNK)
        _gla_chunk(rf, qf_ref, kf_ref, vf_ref, gf_ref, of_ref, sf_ref, cum_f, lower, GLA_CHUNK - 1)
        _gla_chunk(rb, qb_ref, kb_ref, vb_ref, gb_ref, ob_ref, sb_ref, cum_b, upper, 0)
        return carry

    lax.fori_loop(0, n_chunks, body, 0)


def _gla(q, k, v, g, batch, seq, tile):
    n = batch * seq
    nt = seq // tile
    fwd = lambda b, t: (b * nt + t, 0)
    bwd = lambda b, t: (b * nt + (nt - 1 - t), 0)
    bwd_g = lambda b, t: (b * nt + (nt - 1 - t), 1)
    in_specs = [
        pl.BlockSpec((tile, QK_W), fwd), pl.BlockSpec((tile, QK_W), fwd),
        pl.BlockSpec((tile, V_W), fwd), pl.BlockSpec((tile, QK_W), fwd),
        pl.BlockSpec((tile, QK_W), bwd), pl.BlockSpec((tile, QK_W), bwd),
        pl.BlockSpec((tile, V_W), bwd), pl.BlockSpec((tile, QK_W), bwd_g),
    ]
    state = pltpu.VMEM((GLA_HEADS, GLA_DV, GLA_DK), F32)
    block_bytes = 2 * (2 * _nbytes((tile, QK_W), BF16) + 2 * _nbytes((tile, V_W), BF16)
                       + _nbytes((tile, QK_W), F32))
    return pl.pallas_call(
        functools.partial(_gla_kernel, n_chunks=tile // GLA_CHUNK),
        grid=(batch, nt),
        in_specs=in_specs,
        out_specs=[pl.BlockSpec((tile, V_W), fwd), pl.BlockSpec((tile, V_W), bwd)],
        out_shape=[jax.ShapeDtypeStruct((n, V_W), BF16)] * 2,
        scratch_shapes=[state, state],
        compiler_params=pltpu.CompilerParams(
            dimension_semantics=("arbitrary", "arbitrary"),
            vmem_limit_bytes=_vmem_limit(block_bytes, 2 * _nbytes((GLA_HEADS, GLA_DV, GLA_DK), F32))),
        name="gla",
    )(q, k, v, g, q, k, v, g)


def _mix_out_kernel(of_ref, ob_ref, sr_ref, u_ref, up_ref, un_ref, sga_ref, sgb_ref, x_ref,
                    og_ref, wgrp_ref, ps_ref, wa_ref, wb_ref, wo_ref, y_ref, *, tile, seq):
    t = pl.program_id(1)
    nt = pl.num_programs(1)

    o = of_ref[...].astype(F32) + ob_ref[...].astype(F32)
    heads = []
    for h in range(GLA_HEADS):
        oh = o[:, h * GLA_DV:(h + 1) * GLA_DV]
        heads.append(oh * lax.rsqrt(jnp.mean(oh * oh, axis=-1, keepdims=True) + EPS))
    a_in = jnp.concatenate(heads, axis=-1) * og_ref[...] * sr_ref[...].astype(F32)
    a_out = _dot(a_in.astype(BF16), wa_ref[...])

    u_prev = jnp.where(t > 0, up_ref[...].astype(F32), 0.0)
    u_next = jnp.where(t < nt - 1, un_ref[...].astype(F32), 0.0)
    u_ext = jnp.concatenate([u_prev, u_ref[...].astype(F32), u_next], axis=0)
    ext = tile + 2 * HALO
    pos = t * tile + lax.broadcasted_iota(jnp.int32, (tile, POOL_GC), 0)
    pooled = []
    for gi, w in enumerate(POOL_WINDOWS):
        ug = u_ext[:, gi * POOL_GC:(gi + 1) * POOL_GC]
        trail = ug
        span = 1
        while span < w:
            trail = trail + pltpu.roll(trail, span, 0)
            span *= 2
        ahead = w - 1 - w // 2
        centred = pltpu.roll(trail, ext - ahead, 0) if ahead else trail
        wsum = centred[HALO:HALO + tile, :]
        lo = jnp.maximum(pos - w // 2, 0)
        hi = jnp.minimum(pos + ahead, seq - 1)
        cnt = (hi - lo + 1).astype(F32)
        p = wsum / cnt - ug[HALO:HALO + tile, :]
        pooled.append(_dot(p.astype(BF16), wgrp_ref[gi]))
    pb = (jnp.concatenate(pooled, axis=-1) * ps_ref[...]).astype(BF16)
    b_out = _dot(pb, wb_ref[...])

    merged = sga_ref[...].astype(F32) * a_out + sgb_ref[...].astype(F32) * b_out
    y_ref[...] = x_ref[...] + _dot(merged.astype(BF16), wo_ref[...])


def _mix_out(o_f, o_b, sr, u, sga, sgb, x, og, wgrp, ps, wa, wb, wo, batch, seq, tile):
    n = batch * seq
    nt = seq // tile
    hb = tile // HALO
    n_hb = n // HALO
    row = lambda b, t: (b * nt + t, 0)
    prev = lambda b, t: (jnp.maximum((b * nt + t) * hb - 1, 0), 0)
    nxt = lambda b, t: (jnp.minimum((b * nt + t + 1) * hb, n_hb - 1), 0)
    full2 = lambda b, t: (0, 0)
    full3 = lambda b, t: (0, 0, 0)
    in_specs = [
        pl.BlockSpec((tile, V_W), row), pl.BlockSpec((tile, V_W), row), pl.BlockSpec((tile, V_W), row),
        pl.BlockSpec((tile, POOL_WIDTH), row), pl.BlockSpec((HALO, POOL_WIDTH), prev),
        pl.BlockSpec((HALO, POOL_WIDTH), nxt),
        pl.BlockSpec((tile, D_MODEL), row), pl.BlockSpec((tile, D_MODEL), row), pl.BlockSpec((tile, D_MODEL), row),
        pl.BlockSpec((1, V_W), full2), pl.BlockSpec((POOL_GROUPS, POOL_GC, POOL_GC), full3),
        pl.BlockSpec((1, POOL_WIDTH), full2),
        pl.BlockSpec((V_W, D_MODEL), full2), pl.BlockSpec((POOL_WIDTH, D_MODEL), full2),
        pl.BlockSpec((D_MODEL, D_MODEL), full2),
    ]
    block_bytes = (5 * _nbytes((tile, D_MODEL), BF16) + _nbytes((tile + 2 * HALO, POOL_WIDTH), BF16)
                   + 2 * _nbytes((tile, D_MODEL), F32))
    weight_bytes = (_nbytes((V_W + POOL_WIDTH + D_MODEL, D_MODEL), BF16)
                    + _nbytes((POOL_GROUPS, POOL_GC, POOL_GC), BF16))
    return pl.pallas_call(
        functools.partial(_mix_out_kernel, tile=tile, seq=seq),
        grid=(batch, nt),
        in_specs=in_specs,
        out_specs=pl.BlockSpec((tile, D_MODEL), row),
        out_shape=jax.ShapeDtypeStruct((n, D_MODEL), F32),
        compiler_params=pltpu.CompilerParams(
            dimension_semantics=("arbitrary", "arbitrary"),
            vmem_limit_bytes=_vmem_limit(block_bytes, weight_bytes)),
        name="mix_out",
    )(o_f, o_b, sr, u, u, u, sga, sgb, x, og, wgrp, ps, wa, wb, wo)


def _conv_ffn_kernel(x_ref, xp_ref, xn_ref, g2_ref, wup_ref, cw_ref, cb_ref, wdn_ref, gf_ref,
                     y_ref, xe_ref, acc_ref, *, tile, ff_chunk, final_norm):
    t = pl.program_id(1)
    nt = pl.num_programs(1)
    g2 = g2_ref[...]
    x = x_ref[...]
    xe_ref[0:HALO, :] = jnp.where(t > 0, _rmsnorm(xp_ref[...], g2), 0.0).astype(BF16)
    xe_ref[HALO:HALO + tile, :] = _rmsnorm(x, g2).astype(BF16)
    xe_ref[HALO + tile:, :] = jnp.where(t < nt - 1, _rmsnorm(xn_ref[...], g2), 0.0).astype(BF16)
    ext = tile + 2 * HALO

    acc_ref[...] = jnp.zeros_like(acc_ref)
    for j in range(D_FF // ff_chunk):
        cs = slice(j * ff_chunk, (j + 1) * ff_chunk)
        a = _dot(xe_ref[...], wup_ref[:, cs])
        val = _dot(xe_ref[HALO:HALO + tile, :], wup_ref[:, D_FF + j * ff_chunk:D_FF + (j + 1) * ff_chunk])
        cw = cw_ref[:, cs]
        conv = (pltpu.roll(a, 1, 0) * cw[0:1, :] + a * cw[1:2, :]
                + pltpu.roll(a, ext - 1, 0) * cw[2:3, :])[HALO:HALO + tile, :] + cb_ref[:, cs]
        hid = (conv * jax.nn.sigmoid(conv) * val).astype(BF16)
        acc_ref[...] += _dot(hid, wdn_ref[cs, :])

    y = x + acc_ref[...]
    if final_norm:
        y = _rmsnorm(y, gf_ref[...])
    y_ref[...] = y


def _conv_ffn(x, g2, wup, cw, cb, wdn, gf, batch, seq, tile, final_norm):
    n = batch * seq
    nt = seq // tile
    hb = tile // HALO
    n_hb = n // HALO
    ff_chunk = 256
    row = lambda b, t: (b * nt + t, 0)
    prev = lambda b, t: (jnp.maximum((b * nt + t) * hb - 1, 0), 0)
    nxt = lambda b, t: (jnp.minimum((b * nt + t + 1) * hb, n_hb - 1), 0)
    full = lambda b, t: (0, 0)
    in_specs = [
        pl.BlockSpec((tile, D_MODEL), row), pl.BlockSpec((HALO, D_MODEL), prev), pl.BlockSpec((HALO, D_MODEL), nxt),
        pl.BlockSpec((1, D_MODEL), full),
        pl.BlockSpec((D_MODEL, 2 * D_FF), full),
        pl.BlockSpec((3, D_FF), full), pl.BlockSpec((1, D_FF), full),
        pl.BlockSpec((D_FF, D_MODEL), full),
        pl.BlockSpec((1, D_MODEL), full),
    ]
    scratch = [pltpu.VMEM((tile + 2 * HALO, D_MODEL), BF16), pltpu.VMEM((tile, D_MODEL), F32)]
    block_bytes = 2 * _nbytes((tile + HALO, D_MODEL), F32)
    scratch_bytes = (_nbytes((D_MODEL, 2 * D_FF), BF16) + _nbytes((D_FF, D_MODEL), BF16)
                     + _nbytes((tile + 2 * HALO, D_MODEL), BF16) + _nbytes((tile, D_MODEL), F32))
    return pl.pallas_call(
        functools.partial(_conv_ffn_kernel, tile=tile, ff_chunk=ff_chunk, final_norm=final_norm),
        grid=(batch, nt),
        in_specs=in_specs,
        out_specs=pl.BlockSpec((tile, D_MODEL), row),
        out_shape=jax.ShapeDtypeStruct((n, D_MODEL), F32),
        scratch_shapes=scratch,
        compiler_params=pltpu.CompilerParams(
            dimension_semantics=("arbitrary", "arbitrary"),
            vmem_limit_bytes=_vmem_limit(block_bytes, scratch_bytes)),
        name="conv_ffn",
    )(x, x, x, g2, wup, cw, cb, wdn, gf)


def _tiles(seq):
    tok = min(512, seq)
    assert seq % tok == 0 and tok % GLA_CHUNK == 0 and tok % HALO == 0
    return tok


def _prep_layer(l, w_in, w_lr2_f, b_lr_f, w_lr2_b, b_lr_b, w_pool_grp, w_br_a, w_br_b, w_out, w_up, w_down):
    lr0 = 2 * QK_W + 2 * V_W
    lr1 = lr0 + 2 * GLA_LOWRANK
    wl = w_in[l]
    wm = jnp.concatenate([wl[:, :lr0], wl[:, lr1:]], axis=1).astype(BF16)
    wlr = jnp.pad(wl[:, lr0:lr1], ((0, 0), (0, LR_PAD - 2 * GLA_LOWRANK))).astype(BF16)
    wlr2 = jnp.zeros((LR_PAD, 2 * QK_W), F32)
    wlr2 = wlr2.at[:GLA_LOWRANK, :QK_W].set(w_lr2_f[l])
    wlr2 = wlr2.at[GLA_LOWRANK:2 * GLA_LOWRANK, QK_W:].set(w_lr2_b[l]).astype(BF16)
    blr = jnp.concatenate([b_lr_f[l], b_lr_b[l]])[None, :]
    return dict(wm=wm, wlr=wlr, wlr2=wlr2, blr=blr, wgrp=w_pool_grp[l].astype(BF16),
                wa=w_br_a[l].astype(BF16), wb=w_br_b[l].astype(BF16), wo=w_out[l].astype(BF16),
                wup=w_up[l].astype(BF16), wdn=w_down[l].astype(BF16))


def _trunk(x, layers, norm1_g, onorm_g, pool_scale, norm2_g, conv_w, conv_b, final_g):
    batch, seq, _ = x.shape
    tile = _tiles(seq)
    depth = len(layers)
    x = x.reshape(batch * seq, D_MODEL)
    for l, w in enumerate(layers):
        q, k, v, sr, u, sga, sgb, g = _in_proj(x, norm1_g[l][None, :], w["wm"], w["wlr"], w["wlr2"], w["blr"], tile)
        o_f, o_b = _gla(q, k, v, g, batch, seq, tile)
        x = _mix_out(o_f, o_b, sr, u, sga, sgb, x, onorm_g[l][None, :], w["wgrp"], pool_scale[l][None, :],
                     w["wa"], w["wb"], w["wo"], batch, seq, tile)
        x = _conv_ffn(x, norm2_g[l][None, :], w["wup"], conv_w[l], conv_b[l][None, :], w["wdn"],
                      final_g[None, :], batch, seq, tile, final_norm=(l == depth - 1))
    return x.reshape(batch, seq, D_MODEL)


def kernel(x_prompt, x_sample, norm1_g, w_in, w_lr2_f, b_lr_f, w_lr2_b, b_lr_b, onorm_g, w_pool_grp, pool_scale, w_br_a, w_br_b, w_out, norm2_g, w_up, conv_w, conv_b, w_down, final_g):
    depth = w_in.shape[0]
    layers = [_prep_layer(l, w_in, w_lr2_f, b_lr_f, w_lr2_b, b_lr_b, w_pool_grp, w_br_a, w_br_b, w_out,
                          w_up, w_down) for l in range(depth)]
    args = (layers, norm1_g, onorm_g, pool_scale, norm2_g, conv_w, conv_b, final_g)
    return (_trunk(x_prompt, *args), _trunk(x_sample, *args))
```

```python
import functools

import jax
import jax.numpy as jnp
from jax import lax
from jax.experimental import pallas as pl
from jax.experimental.pallas import tpu as pltpu

D_MODEL = 1024
GLA_HEADS = 4
GLA_DK = 128
GLA_DV = 256
QK_W = GLA_HEADS * GLA_DK
V_W = GLA_HEADS * GLA_DV
GLA_LOWRANK = 16
GLA_TAU = 16.0
GLA_BLOCK = 128
POOL_GROUPS = 4
POOL_WIDTH = 512
POOL_GC = 128
POOL_WINDOWS = (2, 4, 8, 16)
D_FF = 2816
EPS = 1e-6

V7X_VMEM_BYTES = 64 * 1024 * 1024
V7X_LANES = 128
V7X_MXU_DIM = 256
BF16_SUBLANE_TILE = 16

LR_PAD = V7X_LANES
HALO = BF16_SUBLANE_TILE
MAIN_W = 2 * QK_W + 2 * V_W + POOL_WIDTH + 2 * D_MODEL
FF_CHUNK = V7X_MXU_DIM
MIX_SUB = 2

F32 = jnp.float32
BF16 = jnp.bfloat16


def _vmem_limit(block_bytes, scratch_bytes):
    want = 2 * block_bytes + scratch_bytes + 16 * 1024 * 1024
    return int(min(want, V7X_VMEM_BYTES - 8 * 1024 * 1024))


def _nbytes(shape, dtype):
    n = 1
    for s in shape:
        n *= s
    return n * jnp.dtype(dtype).itemsize


def _rmsnorm(x, g):
    return x * lax.rsqrt(jnp.mean(x * x, axis=-1, keepdims=True) + EPS) * g


def _dot(a, b):
    return jnp.dot(a, b, preferred_element_type=F32)


def _dot_nt(a, b):
    return lax.dot_general(a, b, (((1,), (1,)), ((), ())), preferred_element_type=F32)


def _in_proj_kernel(x_ref, g1_ref, wm_ref, wlr_ref, wlr2_ref, blr_ref,
                    q_ref, k_ref, v_ref, sr_ref, u_ref, sga_ref, sgb_ref, g_ref):
    xn = _rmsnorm(x_ref[...], g1_ref[...]).astype(BF16)

    def proj(c0, w):
        return _dot(xn, wm_ref[:, c0:c0 + w])

    cw = 512
    lr = _dot(xn, wlr_ref[...]).astype(BF16)
    q_ref[...] = (proj(0, QK_W) * (GLA_DK ** -0.5)).astype(BF16)
    z = _dot(lr, wlr2_ref[...]) + blr_ref[...]
    k_ref[...] = proj(QK_W, QK_W).astype(BF16)
    g_ref[...] = (jnp.minimum(z, 0.0) - jnp.log(1.0 + jnp.exp(-jnp.abs(z)))) * (1.0 / GLA_TAU)
    c = 2 * QK_W
    for j in range(V_W // cw):
        v_ref[:, j * cw:(j + 1) * cw] = proj(c + j * cw, cw).astype(BF16)
    c += V_W
    for j in range(V_W // cw):
        r = proj(c + j * cw, cw)
        sr_ref[:, j * cw:(j + 1) * cw] = (r * jax.nn.sigmoid(r)).astype(BF16)
    c += V_W
    u_ref[...] = proj(c, POOL_WIDTH).astype(BF16)
    c += POOL_WIDTH
    for j in range(D_MODEL // cw):
        sga_ref[:, j * cw:(j + 1) * cw] = jax.nn.sigmoid(proj(c + j * cw, cw)).astype(BF16)
    c += D_MODEL
    for j in range(D_MODEL // cw):
        sgb_ref[:, j * cw:(j + 1) * cw] = jax.nn.sigmoid(proj(c + j * cw, cw)).astype(BF16)


def _in_proj(x, g1, wm, wlr, wlr2, blr, tile):
    n = x.shape[0]
    grid = (n // tile,)
    row = lambda i: (i, 0)
    full = lambda i: (0, 0)
    outs = [(QK_W, BF16), (QK_W, BF16), (V_W, BF16), (V_W, BF16), (POOL_WIDTH, BF16),
            (D_MODEL, BF16), (D_MODEL, BF16), (2 * QK_W, F32)]
    in_specs = [
        pl.BlockSpec((tile, D_MODEL), row),
        pl.BlockSpec((1, D_MODEL), full),
        pl.BlockSpec((D_MODEL, MAIN_W), full),
        pl.BlockSpec((D_MODEL, LR_PAD), full),
        pl.BlockSpec((LR_PAD, 2 * QK_W), full),
        pl.BlockSpec((1, 2 * QK_W), full),
    ]
    block_bytes = _nbytes((tile, D_MODEL), F32) + sum(_nbytes((tile, w), d) for w, d in outs)
    weight_bytes = _nbytes((D_MODEL, MAIN_W + LR_PAD), BF16) + _nbytes((LR_PAD, 2 * QK_W), BF16)
    return pl.pallas_call(
        _in_proj_kernel,
        grid=grid,
        in_specs=in_specs,
        out_specs=[pl.BlockSpec((tile, w), row) for w, _ in outs],
        out_shape=[jax.ShapeDtypeStruct((n, w), d) for w, d in outs],
        compiler_params=pltpu.CompilerParams(
            dimension_semantics=("arbitrary",),
            vmem_limit_bytes=_vmem_limit(block_bytes, weight_bytes)),
        name="in_proj",
    )(x, g1, wm, wlr, wlr2, blr)


def _gla_prep(r0, q_ref, k_ref, v_ref, g_ref, cum, edge_row):
    rows = pl.ds(r0, GLA_BLOCK)
    g = g_ref[rows, :]
    g_hi = g.astype(BF16)
    g_lo = (g - g_hi.astype(F32)).astype(BF16)
    b = _dot(cum, g_hi) + _dot(cum, g_lo)
    s_decay = jnp.exp(b[edge_row:edge_row + 1, :])
    q_dec = (q_ref[rows, :].astype(F32) * jnp.exp(b)).astype(BF16)
    k_inv = k_ref[rows, :].astype(F32) * jnp.exp(-b)
    k_dec = (k_inv * s_decay).astype(BF16)
    v_t = v_ref[rows, :].T
    return q_dec, k_inv.astype(BF16), k_dec, s_decay, v_t


def _gla_kernel(qf_ref, kf_ref, vf_ref, gf_ref, qb_ref, kb_ref, vb_ref, gb_ref,
                of_ref, ob_ref, sf_ref, sb_ref, *, n_blocks):
    @pl.when(pl.program_id(1) == 0)
    def _():
        sf_ref[...] = jnp.zeros_like(sf_ref)
        sb_ref[...] = jnp.zeros_like(sb_ref)

    ri = lax.broadcasted_iota(jnp.int32, (GLA_BLOCK, GLA_BLOCK), 0)
    ci = lax.broadcasted_iota(jnp.int32, (GLA_BLOCK, GLA_BLOCK), 1)
    lower = ri >= ci
    upper = ri <= ci
    cum_f = lower.astype(BF16)
    cum_b = upper.astype(BF16)
    ks = [slice(h * GLA_DK, (h + 1) * GLA_DK) for h in range(GLA_HEADS)]
    vs = [slice(h * GLA_DV, (h + 1) * GLA_DV) for h in range(GLA_HEADS)]
    heads = [(d, h) for d in range(2) for h in range(GLA_HEADS)]

    def body(c, carry):
        rf = pl.multiple_of(c * GLA_BLOCK, GLA_BLOCK)
        rb = pl.multiple_of((n_blocks - 1 - c) * GLA_BLOCK, GLA_BLOCK)
        dirs = (
            (rf, of_ref, sf_ref, lower, _gla_prep(rf, qf_ref, kf_ref, vf_ref, gf_ref, cum_f, GLA_BLOCK - 1)),
            (rb, ob_ref, sb_ref, upper, _gla_prep(rb, qb_ref, kb_ref, vb_ref, gb_ref, cum_b, 0)),
        )
        scores, upd = {}, {}
        for d, h in heads:
            q_dec, k_inv = dirs[d][4][0], dirs[d][4][1]
            scores[d, h] = _dot_nt(q_dec[:, ks[h]], k_inv[:, ks[h]])
        for d, h in heads:
            k_dec, v_t = dirs[d][4][2], dirs[d][4][4]
            upd[d, h] = _dot(v_t[vs[h], :], k_dec[:, ks[h]])
        for d, h in heads:
            r0, o_ref, s_ref, mask, (q_dec, _, _, s_decay, v_t) = dirs[d]
            s = s_ref[h]
            a = jnp.where(mask, scores[d, h], 0.0).astype(BF16)
            lhs = jnp.concatenate([q_dec[:, ks[h]], a], axis=1)
            rhs_t = jnp.concatenate([s.astype(BF16), v_t[vs[h], :]], axis=1)
            o_ref[pl.ds(r0, GLA_BLOCK), vs[h]] = _dot_nt(lhs, rhs_t).astype(o_ref.dtype)
            s_ref[h] = s * s_decay[:, ks[h]] + upd[d, h]
        return carry

    lax.fori_loop(0, n_blocks, body, 0)


def _gla(q, k, v, g, batch, seq, tile):
    n = batch * seq
    nt = seq // tile
    fwd = lambda b, t: (b * nt + t, 0)
    bwd = lambda b, t: (b * nt + (nt - 1 - t), 0)
    bwd_g = lambda b, t: (b * nt + (nt - 1 - t), 1)
    in_specs = [
        pl.BlockSpec((tile, QK_W), fwd), pl.BlockSpec((tile, QK_W), fwd),
        pl.BlockSpec((tile, V_W), fwd), pl.BlockSpec((tile, QK_W), fwd),
        pl.BlockSpec((tile, QK_W), bwd), pl.BlockSpec((tile, QK_W), bwd),
        pl.BlockSpec((tile, V_W), bwd), pl.BlockSpec((tile, QK_W), bwd_g),
    ]
    state = pltpu.VMEM((GLA_HEADS, GLA_DV, GLA_DK), F32)
    block_bytes = 2 * (2 * _nbytes((tile, QK_W), BF16) + 2 * _nbytes((tile, V_W), BF16)
                       + _nbytes((tile, QK_W), F32))
    return pl.pallas_call(
        functools.partial(_gla_kernel, n_blocks=tile // GLA_BLOCK),
        grid=(batch, nt),
        in_specs=in_specs,
        out_specs=[pl.BlockSpec((tile, V_W), fwd), pl.BlockSpec((tile, V_W), bwd)],
        out_shape=[jax.ShapeDtypeStruct((n, V_W), BF16)] * 2,
        scratch_shapes=[state, state],
        compiler_params=pltpu.CompilerParams(
            dimension_semantics=("arbitrary", "arbitrary"),
            vmem_limit_bytes=_vmem_limit(block_bytes, 2 * _nbytes((GLA_HEADS, GLA_DV, GLA_DK), F32))),
        name="gla",
    )(q, k, v, g, q, k, v, g)


def _mix_out_kernel(of_ref, ob_ref, sr_ref, u_ref, up_ref, un_ref, sga_ref, sgb_ref, x_ref,
                    og_ref, wgrp_ref, ps_ref, wa_ref, wb_ref, wo_ref, y_ref, *, tile, seq):
    t = pl.program_id(1)
    nt = pl.num_programs(1)
    sub = tile // MIX_SUB

    u_prev = jnp.where(t > 0, up_ref[...].astype(F32), 0.0)
    u_next = jnp.where(t < nt - 1, un_ref[...].astype(F32), 0.0)
    u_ext = jnp.concatenate([u_prev, u_ref[...].astype(F32), u_next], axis=0)

    def branches(i):
        r0 = i * sub
        rows = slice(r0, r0 + sub)
        ext = sub + 2 * HALO
        pos = t * tile + r0 + lax.broadcasted_iota(jnp.int32, (sub, POOL_GC), 0)
        pooled = []
        for gi, w in enumerate(POOL_WINDOWS):
            ug = u_ext[r0:r0 + ext, gi * POOL_GC:(gi + 1) * POOL_GC]
            trail = ug
            span = 1
            while span < w:
                trail = trail + pltpu.roll(trail, span, 0)
                span *= 2
            ahead = w - 1 - w // 2
            centred = pltpu.roll(trail, ext - ahead, 0) if ahead else trail
            lo = jnp.maximum(pos - w // 2, 0)
            hi = jnp.minimum(pos + ahead, seq - 1)
            cnt = (hi - lo + 1).astype(F32)
            p = centred[HALO:HALO + sub, :] / cnt - ug[HALO:HALO + sub, :]
            pooled.append(_dot(p.astype(BF16), wgrp_ref[gi]))
        pb = (jnp.concatenate(pooled, axis=-1) * ps_ref[...]).astype(BF16)
        b_out = _dot(pb, wb_ref[...])
        o = of_ref[rows, :].astype(F32) + ob_ref[rows, :].astype(F32)
        normed = []
        for h in range(GLA_HEADS):
            oh = o[:, h * GLA_DV:(h + 1) * GLA_DV]
            normed.append(oh * lax.rsqrt(jnp.mean(oh * oh, axis=-1, keepdims=True) + EPS))
        a_in = jnp.concatenate(normed, axis=-1) * og_ref[...] * sr_ref[rows, :].astype(F32)
        a_out = _dot(a_in.astype(BF16), wa_ref[...])
        return a_out, b_out

    def merge(i, a_out, b_out):
        rows = slice(i * sub, (i + 1) * sub)
        merged = sga_ref[rows, :].astype(F32) * a_out + sgb_ref[rows, :].astype(F32) * b_out
        y_ref[rows, :] = x_ref[rows, :] + _dot(merged.astype(BF16), wo_ref[...])

    pending = branches(0)
    for i in range(1, MIX_SUB):
        nxt = branches(i)
        merge(i - 1, *pending)
        pending = nxt
    merge(MIX_SUB - 1, *pending)


def _mix_out(o_f, o_b, sr, u, sga, sgb, x, og, wgrp, ps, wa, wb, wo, batch, seq, tile):
    n = batch * seq
    nt = seq // tile
    hb = tile // HALO
    n_hb = n // HALO
    row = lambda b, t: (b * nt + t, 0)
    prev = lambda b, t: (jnp.maximum((b * nt + t) * hb - 1, 0), 0)
    nxt = lambda b, t: (jnp.minimum((b * nt + t + 1) * hb, n_hb - 1), 0)
    full2 = lambda b, t: (0, 0)
    full3 = lambda b, t: (0, 0, 0)
    in_specs = [
        pl.BlockSpec((tile, V_W), row), pl.BlockSpec((tile, V_W), row), pl.BlockSpec((tile, V_W), row),
        pl.BlockSpec((tile, POOL_WIDTH), row), pl.BlockSpec((HALO, POOL_WIDTH), prev),
        pl.BlockSpec((HALO, POOL_WIDTH), nxt),
        pl.BlockSpec((tile, D_MODEL), row), pl.BlockSpec((tile, D_MODEL), row), pl.BlockSpec((tile, D_MODEL), row),
        pl.BlockSpec((1, V_W), full2), pl.BlockSpec((POOL_GROUPS, POOL_GC, POOL_GC), full3),
        pl.BlockSpec((1, POOL_WIDTH), full2),
        pl.BlockSpec((V_W, D_MODEL), full2), pl.BlockSpec((POOL_WIDTH, D_MODEL), full2),
        pl.BlockSpec((D_MODEL, D_MODEL), full2),
    ]
    block_bytes = (5 * _nbytes((tile, D_MODEL), BF16) + _nbytes((tile + 2 * HALO, POOL_WIDTH), BF16)
                   + 2 * _nbytes((tile, D_MODEL), F32))
    weight_bytes = (_nbytes((V_W + POOL_WIDTH + D_MODEL, D_MODEL), BF16)
                    + _nbytes((POOL_GROUPS, POOL_GC, POOL_GC), BF16))
    return pl.pallas_call(
        functools.partial(_mix_out_kernel, tile=tile, seq=seq),
        grid=(batch, nt),
        in_specs=in_specs,
        out_specs=pl.BlockSpec((tile, D_MODEL), row),
        out_shape=jax.ShapeDtypeStruct((n, D_MODEL), F32),
        compiler_params=pltpu.CompilerParams(
            dimension_semantics=("arbitrary", "arbitrary"),
            vmem_limit_bytes=_vmem_limit(block_bytes, weight_bytes)),
        name="mix_out",
    )(o_f, o_b, sr, u, u, u, sga, sgb, x, og, wgrp, ps, wa, wb, wo)


def _conv_ffn_kernel(x_ref, xp_ref, xn_ref, g2_ref, wup_ref, cw_ref, cb_ref, wdn_ref, gf_ref,
                     y_ref, xe_ref, hid_ref, *, tile, final_norm):
    t = pl.program_id(1)
    nt = pl.num_programs(1)
    g2 = g2_ref[...]
    x = x_ref[...]
    xe_ref[0:HALO, :] = jnp.where(t > 0, _rmsnorm(xp_ref[...], g2), 0.0).astype(BF16)
    xe_ref[HALO:HALO + tile, :] = _rmsnorm(x, g2).astype(BF16)
    xe_ref[HALO + tile:, :] = jnp.where(t < nt - 1, _rmsnorm(xn_ref[...], g2), 0.0).astype(BF16)
    ext = tile + 2 * HALO
    n_chunks = D_FF // FF_CHUNK

    def up(j):
        a = _dot(xe_ref[...], wup_ref[:, j * FF_CHUNK:(j + 1) * FF_CHUNK])
        val = _dot(xe_ref[HALO:HALO + tile, :], wup_ref[:, D_FF + j * FF_CHUNK:D_FF + (j + 1) * FF_CHUNK])
        return a, val

    def act(j, a, val):
        cs = slice(j * FF_CHUNK, (j + 1) * FF_CHUNK)
        cw = cw_ref[:, cs]
        conv = (pltpu.roll(a, 1, 0) * cw[0:1, :] + a * cw[1:2, :]
                + pltpu.roll(a, ext - 1, 0) * cw[2:3, :])[HALO:HALO + tile, :] + cb_ref[:, cs]
        hid_ref[:, cs] = (conv * jax.nn.sigmoid(conv) * val).astype(BF16)

    split = (n_chunks - 2) * FF_CHUNK
    nxt = up(0)
    down = None
    for j in range(n_chunks):
        cur = nxt
        if j + 1 < n_chunks:
            nxt = up(j + 1)
        if j == n_chunks - 2:
            down = _dot(hid_ref[:, :split], wdn_ref[:split, :])
        act(j, *cur)
    y = x + (down + _dot(hid_ref[:, split:], wdn_ref[split:, :]))
    if final_norm:
        y = _rmsnorm(y, gf_ref[...])
    y_ref[...] = y


def _conv_ffn(x, g2, wup, cw, cb, wdn, gf, batch, seq, tile, final_norm):
    n = batch * seq
    nt = seq // tile
    hb = tile // HALO
    n_hb = n // HALO
    row = lambda b, t: (b * nt + t, 0)
    prev = lambda b, t: (jnp.maximum((b * nt + t) * hb - 1, 0), 0)
    nxt = lambda b, t: (jnp.minimum((b * nt + t + 1) * hb, n_hb - 1), 0)
    full = lambda b, t: (0, 0)
    in_specs = [
        pl.BlockSpec((tile, D_MODEL), row), pl.BlockSpec((HALO, D_MODEL), prev), pl.BlockSpec((HALO, D_MODEL), nxt),
        pl.BlockSpec((1, D_MODEL), full),
        pl.BlockSpec((D_MODEL, 2 * D_FF), full),
        pl.BlockSpec((3, D_FF), full), pl.BlockSpec((1, D_FF), full),
        pl.BlockSpec((D_FF, D_MODEL), full),
        pl.BlockSpec((1, D_MODEL), full),
    ]
    scratch = [pltpu.VMEM((tile + 2 * HALO, D_MODEL), BF16), pltpu.VMEM((tile, D_FF), BF16)]
    block_bytes = 2 * _nbytes((tile + HALO, D_MODEL), F32)
    scratch_bytes = (_nbytes((D_MODEL, 2 * D_FF), BF16) + _nbytes((D_FF, D_MODEL), BF16)
                     + _nbytes((tile + 2 * HALO, D_MODEL), BF16) + _nbytes((tile, D_FF), BF16))
    return pl.pallas_call(
        functools.partial(_conv_ffn_kernel, tile=tile, final_norm=final_norm),
        grid=(batch, nt),
        in_specs=in_specs,
        out_specs=pl.BlockSpec((tile, D_MODEL), row),
        out_shape=jax.ShapeDtypeStruct((n, D_MODEL), F32),
        scratch_shapes=scratch,
        compiler_params=pltpu.CompilerParams(
            dimension_semantics=("arbitrary", "arbitrary"),
            vmem_limit_bytes=_vmem_limit(block_bytes, scratch_bytes)),
        name="conv_ffn",
    )(x, x, x, g2, wup, cw, cb, wdn, gf)


def _tiles(seq):
    tok = min(512, seq)
    assert seq % tok == 0 and tok % GLA_BLOCK == 0 and tok % (HALO * MIX_SUB) == 0
    return tok


def _prep_layer(l, w_in, w_lr2_f, b_lr_f, w_lr2_b, b_lr_b, w_pool_grp, w_br_a, w_br_b, w_out, w_up, w_down):
    lr0 = 2 * QK_W + 2 * V_W
    lr1 = lr0 + 2 * GLA_LOWRANK
    wl = w_in[l]
    wm = jnp.concatenate([wl[:, :lr0], wl[:, lr1:]], axis=1).astype(BF16)
    wlr = jnp.pad(wl[:, lr0:lr1], ((0, 0), (0, LR_PAD - 2 * GLA_LOWRANK))).astype(BF16)
    wlr2 = jnp.zeros((LR_PAD, 2 * QK_W), F32)
    wlr2 = wlr2.at[:GLA_LOWRANK, :QK_W].set(w_lr2_f[l])
    wlr2 = wlr2.at[GLA_LOWRANK:2 * GLA_LOWRANK, QK_W:].set(w_lr2_b[l]).astype(BF16)
    blr = jnp.concatenate([b_lr_f[l], b_lr_b[l]])[None, :]
    return dict(wm=wm, wlr=wlr, wlr2=wlr2, blr=blr, wgrp=w_pool_grp[l].astype(BF16),
                wa=w_br_a[l].astype(BF16), wb=w_br_b[l].astype(BF16), wo=w_out[l].astype(BF16),
                wup=w_up[l].astype(BF16), wdn=w_down[l].astype(BF16))


def _trunk(x, layers, norm1_g, onorm_g, pool_scale, norm2_g, conv_w, conv_b, final_g):
    batch, seq, _ = x.shape
    tile = _tiles(seq)
    depth = len(layers)
    x = x.reshape(batch * seq, D_MODEL)
    for l, w in enumerate(layers):
        q, k, v, sr, u, sga, sgb, g = _in_proj(x, norm1_g[l][None, :], w["wm"], w["wlr"], w["wlr2"], w["blr"], tile)
        o_f, o_b = _gla(q, k, v, g, batch, seq, tile)
        x = _mix_out(o_f, o_b, sr, u, sga, sgb, x, onorm_g[l][None, :], w["wgrp"], pool_scale[l][None, :],
                     w["wa"], w["wb"], w["wo"], batch, seq, tile)
        x = _conv_ffn(x, norm2_g[l][None, :], w["wup"], conv_w[l], conv_b[l][None, :], w["wdn"],
                      final_g[None, :], batch, seq, tile, final_norm=(l == depth - 1))
    return x.reshape(batch, seq, D_MODEL)


def kernel(x_prompt, x_sample, norm1_g, w_in, w_lr2_f, b_lr_f, w_lr2_b, b_lr_b, onorm_g, w_pool_grp, pool_scale, w_br_a, w_br_b, w_out, norm2_g, w_up, conv_w, conv_b, w_down, final_g):
    depth = w_in.shape[0]
    layers = [_prep_layer(l, w_in, w_lr2_f, b_lr_f, w_lr2_b, b_lr_b, w_pool_grp, w_br_a, w_br_b, w_out,
                          w_up, w_down) for l in range(depth)]
    args = (layers, norm1_g, onorm_g, pool_scale, norm2_g, conv_w, conv_b, final_g)
    return (_trunk(x_prompt, *args), _trunk(x_sample, *args))
```

```python
import functools

import jax
import jax.numpy as jnp
from jax import lax
from jax.experimental import pallas as pl
from jax.experimental.pallas import tpu as pltpu

D_MODEL = 1024
GLA_HEADS = 4
GLA_DK = 128
GLA_DV = 256
QK_W = GLA_HEADS * GLA_DK
V_W = GLA_HEADS * GLA_DV
GLA_LOWRANK = 16
GLA_TAU = 16.0
GLA_BLOCK = 128
POOL_GROUPS = 4
POOL_WIDTH = 512
POOL_GC = 128
POOL_WINDOWS = (2, 4, 8, 16)
D_FF = 2816
EPS = 1e-6

V7X_VMEM_BYTES = 64 * 1024 * 1024
V7X_LANES = 128
V7X_MXU_DIM = 256
BF16_SUBLANE_TILE = 16
F32_SUBLANE_TILE = 8

LR_PAD = V7X_LANES
HALO = BF16_SUBLANE_TILE
MAIN_W = 2 * QK_W + 2 * V_W + POOL_WIDTH + 2 * D_MODEL
FF_CHUNK = V7X_MXU_DIM
ROW_SUB = 2
EXP_RANGE = 80.0

F32 = jnp.float32
BF16 = jnp.bfloat16


def _vmem_limit(block_bytes, scratch_bytes):
    want = 2 * block_bytes + scratch_bytes + 16 * 1024 * 1024
    return int(min(want, V7X_VMEM_BYTES - 8 * 1024 * 1024))


def _nbytes(shape, dtype):
    n = 1
    for s in shape:
        n *= s
    return n * jnp.dtype(dtype).itemsize


def _rmsnorm(x, g):
    return x * lax.rsqrt(jnp.mean(x * x, axis=-1, keepdims=True) + EPS) * g


def _dot(a, b):
    return jnp.dot(a, b, preferred_element_type=F32)


def _dot_nt(a, b):
    return lax.dot_general(a, b, (((1,), (1,)), ((), ())), preferred_element_type=F32)


def _in_proj_kernel(x_ref, g1_ref, wm_ref, wlr_ref, wlr2_ref, blr_ref,
                    q_ref, k_ref, v_ref, sr_ref, u_ref, sga_ref, sgb_ref, g_ref, gmin_ref, *, tile):
    sub = tile // ROW_SUB
    cw = 512
    g_min = None
    for i in range(ROW_SUB):
        rows = slice(i * sub, (i + 1) * sub)
        xn = _rmsnorm(x_ref[rows, :], g1_ref[...]).astype(BF16)

        def proj(c0, w):
            return _dot(xn, wm_ref[:, c0:c0 + w])

        lr = _dot(xn, wlr_ref[...]).astype(BF16)
        q_ref[rows, :] = (proj(0, QK_W) * (GLA_DK ** -0.5)).astype(BF16)
        k_ref[rows, :] = proj(QK_W, QK_W).astype(BF16)
        c = 2 * QK_W
        for j in range(V_W // cw):
            v_ref[rows, j * cw:(j + 1) * cw] = proj(c + j * cw, cw).astype(BF16)
        c += V_W
        z = _dot(lr, wlr2_ref[...]) + blr_ref[...]
        g = (jnp.minimum(z, 0.0) - jnp.log(1.0 + jnp.exp(-jnp.abs(z)))) * (1.0 / GLA_TAU)
        g_ref[rows, :] = g
        m = jnp.min(g)
        g_min = m if g_min is None else jnp.minimum(g_min, m)
        for j in range(V_W // cw):
            r = proj(c + j * cw, cw)
            sr_ref[rows, j * cw:(j + 1) * cw] = (r * jax.nn.sigmoid(r)).astype(BF16)
        c += V_W
        u_ref[rows, :] = proj(c, POOL_WIDTH).astype(BF16)
        c += POOL_WIDTH
        for j in range(D_MODEL // cw):
            sga_ref[rows, j * cw:(j + 1) * cw] = jax.nn.sigmoid(proj(c + j * cw, cw)).astype(BF16)
        c += D_MODEL
        for j in range(D_MODEL // cw):
            sgb_ref[rows, j * cw:(j + 1) * cw] = jax.nn.sigmoid(proj(c + j * cw, cw)).astype(BF16)
    gmin_ref[...] = jnp.full(gmin_ref.shape, g_min, F32)


def _in_proj(x, g1, wm, wlr, wlr2, blr, tile):
    n = x.shape[0]
    grid = (n // tile,)
    row = lambda i: (i, 0)
    full = lambda i: (0, 0)
    outs = [(QK_W, BF16), (QK_W, BF16), (V_W, BF16), (V_W, BF16), (POOL_WIDTH, BF16),
            (D_MODEL, BF16), (D_MODEL, BF16), (2 * QK_W, F32)]
    in_specs = [
        pl.BlockSpec((tile, D_MODEL), row),
        pl.BlockSpec((1, D_MODEL), full),
        pl.BlockSpec((D_MODEL, MAIN_W), full),
        pl.BlockSpec((D_MODEL, LR_PAD), full),
        pl.BlockSpec((LR_PAD, 2 * QK_W), full),
        pl.BlockSpec((1, 2 * QK_W), full),
    ]
    gmin_block = (1, F32_SUBLANE_TILE, V7X_LANES)
    block_bytes = _nbytes((tile, D_MODEL), F32) + sum(_nbytes((tile, w), d) for w, d in outs)
    weight_bytes = _nbytes((D_MODEL, MAIN_W + LR_PAD), BF16) + _nbytes((LR_PAD, 2 * QK_W), BF16)
    return pl.pallas_call(
        functools.partial(_in_proj_kernel, tile=tile),
        grid=grid,
        in_specs=in_specs,
        out_specs=[pl.BlockSpec((tile, w), row) for w, _ in outs]
        + [pl.BlockSpec(gmin_block, lambda i: (i, 0, 0))],
        out_shape=[jax.ShapeDtypeStruct((n, w), d) for w, d in outs]
        + [jax.ShapeDtypeStruct((n // tile,) + gmin_block[1:], F32)],
        compiler_params=pltpu.CompilerParams(
            dimension_semantics=("arbitrary",),
            vmem_limit_bytes=_vmem_limit(block_bytes, weight_bytes)),
        name="in_proj",
    )(x, g1, wm, wlr, wlr2, blr)


def _cum_log_decay(g, cum):
    g_hi = g.astype(BF16)
    g_lo = (g - g_hi.astype(F32)).astype(BF16)
    return _dot(cum, g_hi) + _dot(cum, g_lo)


def _decay_columns(s_decay):
    return jnp.broadcast_to(s_decay, (GLA_BLOCK, QK_W)).T


def _scale_state(s, dec_t):
    return s * jnp.concatenate([dec_t] * (GLA_DV // GLA_BLOCK), axis=1)


def _gla_prep(r0, b, q_ref, k_ref, v_ref, edge_row):
    rows = slice(r0, r0 + GLA_BLOCK)
    s_decay = jnp.exp(b[edge_row:edge_row + 1, :])
    q_dec = (q_ref[rows, :].astype(F32) * jnp.exp(b)).astype(BF16)
    k_inv = k_ref[rows, :].astype(F32) * jnp.exp(-b)
    k_dec = (k_inv * s_decay).astype(BF16)
    return q_dec, k_inv.astype(BF16), k_dec.T, _decay_columns(s_decay), v_ref[rows, :]


def _gla_fast(fwd_refs, bwd_refs, of_ref, ob_ref, sf_ref, sb_ref, cums, masks, n_blocks):
    ks = [slice(h * GLA_DK, (h + 1) * GLA_DK) for h in range(GLA_HEADS)]
    vs = [slice(h * GLA_DV, (h + 1) * GLA_DV) for h in range(GLA_HEADS)]
    heads = [(d, h) for d in range(2) for h in range(GLA_HEADS)]
    o_refs = (of_ref, ob_ref)
    state = {(d, h): (sf_ref, sb_ref)[d][h] for d, h in heads}

    def outputs(blocks, scores, pre_state):
        for d, h in heads:
            r0, (q_dec, _, _, _, v) = blocks[d]
            a = jnp.where(masks[d], scores[d, h], 0.0).astype(BF16)
            lhs = jnp.concatenate([q_dec[:, ks[h]], a], axis=1)
            rhs = jnp.concatenate([pre_state[d, h].astype(BF16), v[:, vs[h]]], axis=0)
            o_refs[d][r0:r0 + GLA_BLOCK, vs[h]] = _dot(lhs, rhs).astype(o_refs[d].dtype)

    offs = [c * GLA_BLOCK for c in range(n_blocks)]
    b_f = [_cum_log_decay(fwd_refs[3][r0:r0 + GLA_BLOCK, :], cums[0]) for r0 in offs]
    b_b = [_cum_log_decay(bwd_refs[3][r0:r0 + GLA_BLOCK, :], cums[1]) for r0 in offs]
    pending = None
    for step in range(n_blocks):
        cf, cb = step, n_blocks - 1 - step
        rf, rb = offs[cf], offs[cb]
        blocks = ((rf, _gla_prep(rf, b_f[cf], *fwd_refs[:3], GLA_BLOCK - 1)),
                  (rb, _gla_prep(rb, b_b[cb], *bwd_refs[:3], 0)))
        scores, upd = {}, {}
        for d, h in heads:
            q_dec, k_inv = blocks[d][1][0], blocks[d][1][1]
            scores[d, h] = _dot_nt(q_dec[:, ks[h]], k_inv[:, ks[h]])
        for d, h in heads:
            kd_t, v = blocks[d][1][2], blocks[d][1][4]
            upd[d, h] = _dot(kd_t[ks[h], :], v[:, vs[h]])
        if pending is not None:
            outputs(*pending)
        pending = (blocks, scores, dict(state))
        for d, h in heads:
            dec_t = blocks[d][1][3]
            state[d, h] = _scale_state(state[d, h], dec_t[ks[h], :]) + upd[d, h]
    outputs(*pending)
    for d, h in heads:
        (sf_ref, sb_ref)[d][h] = state[d, h]


def _gla_exact(refs, o_ref, s_ref, cum, mask, edge_row, reverse, n_blocks, b_scr, q_scr, k_scr, sc_scr):
    q_ref, k_ref, v_ref, g_ref = refs
    ks = [slice(h * GLA_DK, (h + 1) * GLA_DK) for h in range(GLA_HEADS)]
    vs = [slice(h * GLA_DV, (h + 1) * GLA_DV) for h in range(GLA_HEADS)]
    lane = lax.broadcasted_iota(jnp.int32, (GLA_BLOCK, GLA_BLOCK), 1)

    def block(c, carry):
        blk = (n_blocks - 1 - c) if reverse else c
        rows = pl.ds(pl.multiple_of(blk * GLA_BLOCK, GLA_BLOCK), GLA_BLOCK)
        b = _cum_log_decay(g_ref[rows, :], cum)
        q = q_ref[rows, :].astype(F32)
        k = k_ref[rows, :].astype(F32)
        b_scr[...] = b
        q_scr[...] = q
        k_scr[...] = k
        sc_scr[...] = jnp.zeros_like(sc_scr)

        def column(j, carry):
            w = jnp.exp(jnp.minimum(b_scr[...] - b_scr[pl.ds(j, 1), :], 0.0))
            contrib = q_scr[...] * w * k_scr[pl.ds(j, 1), :]
            for h in range(GLA_HEADS):
                col = jnp.sum(contrib[:, ks[h]], axis=1, keepdims=True)
                sc_scr[h] += jnp.where(lane == j, col, 0.0)
            return carry

        lax.fori_loop(0, GLA_BLOCK, column, 0)
        b_edge = b[edge_row:edge_row + 1, :]
        s_decay = jnp.exp(b_edge)
        q_dec = (q * jnp.exp(b)).astype(BF16)
        kd_t = (k * jnp.exp(b_edge - b)).astype(BF16).T
        dec_t = _decay_columns(s_decay)
        v = v_ref[rows, :]
        for h in range(GLA_HEADS):
            s = s_ref[h]
            a = jnp.where(mask, sc_scr[h], 0.0).astype(BF16)
            lhs = jnp.concatenate([q_dec[:, ks[h]], a], axis=1)
            rhs = jnp.concatenate([s.astype(BF16), v[:, vs[h]]], axis=0)
            o_ref[rows, vs[h]] = _dot(lhs, rhs).astype(o_ref.dtype)
            s_ref[h] = _scale_state(s, dec_t[ks[h], :]) + _dot(kd_t[ks[h], :], v[:, vs[h]])
        return carry

    lax.fori_loop(0, n_blocks, block, 0)


def _gla_kernel(gmin_ref, qf_ref, kf_ref, vf_ref, gf_ref, qb_ref, kb_ref, vb_ref, gb_ref,
                of_ref, ob_ref, sf_ref, sb_ref, b_scr, q_scr, k_scr, sc_scr, *, n_blocks):
    bi = pl.program_id(0)
    t = pl.program_id(1)
    nt = pl.num_programs(1)

    @pl.when(t == 0)
    def _():
        sf_ref[...] = jnp.zeros_like(sf_ref)
        sb_ref[...] = jnp.zeros_like(sb_ref)

    ri = lax.broadcasted_iota(jnp.int32, (GLA_BLOCK, GLA_BLOCK), 0)
    ci = lax.broadcasted_iota(jnp.int32, (GLA_BLOCK, GLA_BLOCK), 1)
    masks = (ri >= ci, ri <= ci)
    cums = (masks[0].astype(BF16), masks[1].astype(BF16))
    fwd_refs = (qf_ref, kf_ref, vf_ref, gf_ref)
    bwd_refs = (qb_ref, kb_ref, vb_ref, gb_ref)

    g_min = jnp.minimum(gmin_ref[bi * nt + t], gmin_ref[bi * nt + (nt - 1 - t)])
    factorisable = g_min * GLA_BLOCK > -EXP_RANGE

    @pl.when(factorisable)
    def _():
        _gla_fast(fwd_refs, bwd_refs, of_ref, ob_ref, sf_ref, sb_ref, cums, masks, n_blocks)

    @pl.when(jnp.logical_not(factorisable))
    def _():
        scr = (b_scr, q_scr, k_scr, sc_scr)
        _gla_exact(fwd_refs, of_ref, sf_ref, cums[0], masks[0], GLA_BLOCK - 1, False, n_blocks, *scr)
        _gla_exact(bwd_refs, ob_ref, sb_ref, cums[1], masks[1], 0, True, n_blocks, *scr)


def _gla(q, k, v, g, g_min, batch, seq, tile):
    n = batch * seq
    nt = seq // tile
    fwd = lambda b, t, gm: (b * nt + t, 0)
    bwd = lambda b, t, gm: (b * nt + (nt - 1 - t), 0)
    bwd_g = lambda b, t, gm: (b * nt + (nt - 1 - t), 1)
    in_specs = [
        pl.BlockSpec((tile, QK_W), fwd), pl.BlockSpec((tile, QK_W), fwd),
        pl.BlockSpec((tile, V_W), fwd), pl.BlockSpec((tile, QK_W), fwd),
        pl.BlockSpec((tile, QK_W), bwd), pl.BlockSpec((tile, QK_W), bwd),
        pl.BlockSpec((tile, V_W), bwd), pl.BlockSpec((tile, QK_W), bwd_g),
    ]
    state = pltpu.VMEM((GLA_HEADS, GLA_DK, GLA_DV), F32)
    rows_f32 = pltpu.VMEM((GLA_BLOCK, QK_W), F32)
    scratch = [state, state, rows_f32, rows_f32, rows_f32, pltpu.VMEM((GLA_HEADS, GLA_BLOCK, GLA_BLOCK), F32)]
    block_bytes = 2 * (2 * _nbytes((tile, QK_W), BF16) + 2 * _nbytes((tile, V_W), BF16)
                       + _nbytes((tile, QK_W), F32))
    scratch_bytes = (2 * _nbytes((GLA_HEADS, GLA_DV, GLA_DK), F32) + 3 * _nbytes((GLA_BLOCK, QK_W), F32)
                     + _nbytes((GLA_HEADS, GLA_BLOCK, GLA_BLOCK), F32))
    grid_spec = pltpu.PrefetchScalarGridSpec(
        num_scalar_prefetch=1,
        grid=(batch, nt),
        in_specs=in_specs,
        out_specs=[pl.BlockSpec((tile, V_W), fwd), pl.BlockSpec((tile, V_W), bwd)],
        scratch_shapes=scratch,
    )
    return pl.pallas_call(
        functools.partial(_gla_kernel, n_blocks=tile // GLA_BLOCK),
        grid_spec=grid_spec,
        out_shape=[jax.ShapeDtypeStruct((n, V_W), BF16)] * 2,
        compiler_params=pltpu.CompilerParams(
            dimension_semantics=("arbitrary", "arbitrary"),
            vmem_limit_bytes=_vmem_limit(block_bytes, scratch_bytes)),
        name="gla",
    )(g_min, q, k, v, g, q, k, v, g)


def _mix_out_kernel(of_ref, ob_ref, sr_ref, u_ref, up_ref, un_ref, sga_ref, sgb_ref, x_ref,
                    og_ref, wgrp_ref, ps_ref, wa_ref, wb_ref, wo_ref, y_ref, *, tile, seq):
    t = pl.program_id(1)
    nt = pl.num_programs(1)
    sub = tile // ROW_SUB

    u_prev = jnp.where(t > 0, up_ref[...].astype(F32), 0.0)
    u_next = jnp.where(t < nt - 1, un_ref[...].astype(F32), 0.0)
    u_ext = jnp.concatenate([u_prev, u_ref[...].astype(F32), u_next], axis=0)

    def window_mean(centred, w, ahead, r0):
        def edge(e0):
            pos = t * tile + r0 + e0 + lax.broadcasted_iota(jnp.int32, (HALO, POOL_GC), 0)
            cnt = jnp.minimum(pos + ahead, seq - 1) - jnp.maximum(pos - w // 2, 0) + 1
            return centred[e0:e0 + HALO, :] / cnt.astype(F32)
        inner = centred[HALO:sub - HALO, :] * (1.0 / w)
        return jnp.concatenate([edge(0), inner, edge(sub - HALO)], axis=0)

    def branches(i):
        r0 = i * sub
        rows = slice(r0, r0 + sub)
        ext = sub + 2 * HALO
        pooled = []
        for gi, w in enumerate(POOL_WINDOWS):
            ug = u_ext[r0:r0 + ext, gi * POOL_GC:(gi + 1) * POOL_GC]
            trail = ug
            span = 1
            while span < w:
                trail = trail + pltpu.roll(trail, span, 0)
                span *= 2
            ahead = w - 1 - w // 2
            centred = pltpu.roll(trail, ext - ahead, 0) if ahead else trail
            p = window_mean(centred[HALO:HALO + sub, :], w, ahead, r0) - ug[HALO:HALO + sub, :]
            pooled.append(_dot(p.astype(BF16), wgrp_ref[gi]))
        pb = (jnp.concatenate(pooled, axis=-1) * ps_ref[...]).astype(BF16)
        b_out = _dot(pb, wb_ref[...])
        o = of_ref[rows, :].astype(F32) + ob_ref[rows, :].astype(F32)
        normed = []
        for h in range(GLA_HEADS):
            oh = o[:, h * GLA_DV:(h + 1) * GLA_DV]
            normed.append(oh * lax.rsqrt(jnp.mean(oh * oh, axis=-1, keepdims=True) + EPS))
        a_in = (jnp.concatenate(normed, axis=-1) * og_ref[...]).astype(BF16) * sr_ref[rows, :]
        a_out = _dot(a_in, wa_ref[...])
        return a_out, b_out

    def merge(i, a_out, b_out):
        rows = slice(i * sub, (i + 1) * sub)
        merged = sga_ref[rows, :] * a_out.astype(BF16) + sgb_ref[rows, :] * b_out.astype(BF16)
        y_ref[rows, :] = x_ref[rows, :] + _dot(merged, wo_ref[...])

    pending = branches(0)
    for i in range(1, ROW_SUB):
        nxt = branches(i)
        merge(i - 1, *pending)
        pending = nxt
    merge(ROW_SUB - 1, *pending)


def _mix_out(o_f, o_b, sr, u, sga, sgb, x, og, wgrp, ps, wa, wb, wo, batch, seq, tile):
    n = batch * seq
    nt = seq // tile
    hb = tile // HALO
    n_hb = n // HALO
    row = lambda b, t: (b * nt + t, 0)
    prev = lambda b, t: (jnp.maximum((b * nt + t) * hb - 1, 0), 0)
    nxt = lambda b, t: (jnp.minimum((b * nt + t + 1) * hb, n_hb - 1), 0)
    full2 = lambda b, t: (0, 0)
    full3 = lambda b, t: (0, 0, 0)
    in_specs = [
        pl.BlockSpec((tile, V_W), row), pl.BlockSpec((tile, V_W), row), pl.BlockSpec((tile, V_W), row),
        pl.BlockSpec((tile, POOL_WIDTH), row), pl.BlockSpec((HALO, POOL_WIDTH), prev),
        pl.BlockSpec((HALO, POOL_WIDTH), nxt),
        pl.BlockSpec((tile, D_MODEL), row), pl.BlockSpec((tile, D_MODEL), row), pl.BlockSpec((tile, D_MODEL), row),
        pl.BlockSpec((1, V_W), full2), pl.BlockSpec((POOL_GROUPS, POOL_GC, POOL_GC), full3),
        pl.BlockSpec((1, POOL_WIDTH), full2),
        pl.BlockSpec((V_W, D_MODEL), full2), pl.BlockSpec((POOL_WIDTH, D_MODEL), full2),
        pl.BlockSpec((D_MODEL, D_MODEL), full2),
    ]
    block_bytes = (5 * _nbytes((tile, D_MODEL), BF16) + _nbytes((tile + 2 * HALO, POOL_WIDTH), BF16)
                   + 2 * _nbytes((tile, D_MODEL), F32))
    weight_bytes = (_nbytes((V_W + POOL_WIDTH + D_MODEL, D_MODEL), BF16)
                    + _nbytes((POOL_GROUPS, POOL_GC, POOL_GC), BF16))
    return pl.pallas_call(
        functools.partial(_mix_out_kernel, tile=tile, seq=seq),
        grid=(batch, nt),
        in_specs=in_specs,
        out_specs=pl.BlockSpec((tile, D_MODEL), row),
        out_shape=jax.ShapeDtypeStruct((n, D_MODEL), F32),
        compiler_params=pltpu.CompilerParams(
            dimension_semantics=("arbitrary", "arbitrary"),
            vmem_limit_bytes=_vmem_limit(block_bytes, weight_bytes)),
        name="mix_out",
    )(o_f, o_b, sr, u, u, u, sga, sgb, x, og, wgrp, ps, wa, wb, wo)


def _conv_ffn_kernel(x_ref, xp_ref, xn_ref, g2_ref, wup_ref, cw_ref, cb_ref, wdn_ref, gf_ref,
                     y_ref, xe_ref, hid_ref, *, tile, final_norm):
    t = pl.program_id(1)
    nt = pl.num_programs(1)
    g2 = g2_ref[...]
    x = x_ref[...]
    xe_ref[0:HALO, :] = jnp.where(t > 0, _rmsnorm(xp_ref[...], g2), 0.0).astype(BF16)
    xe_ref[HALO:HALO + tile, :] = _rmsnorm(x, g2).astype(BF16)
    xe_ref[HALO + tile:, :] = jnp.where(t < nt - 1, _rmsnorm(xn_ref[...], g2), 0.0).astype(BF16)
    ext = tile + 2 * HALO
    n_chunks = D_FF // FF_CHUNK

    def up(j):
        a = _dot(xe_ref[...], wup_ref[:, j * FF_CHUNK:(j + 1) * FF_CHUNK])
        val = _dot(xe_ref[HALO:HALO + tile, :], wup_ref[:, D_FF + j * FF_CHUNK:D_FF + (j + 1) * FF_CHUNK])
        return a, val

    def act(j, a, val):
        cs = slice(j * FF_CHUNK, (j + 1) * FF_CHUNK)
        cw = cw_ref[:, cs]
        conv = (pltpu.roll(a, 1, 0) * cw[0:1, :] + a * cw[1:2, :]
                + pltpu.roll(a, ext - 1, 0) * cw[2:3, :])[HALO:HALO + tile, :] + cb_ref[:, cs]
        hid_ref[:, cs] = (conv * jax.nn.sigmoid(conv) * val).astype(BF16)

    split = (n_chunks - 2) * FF_CHUNK
    nxt = up(0)
    down = None
    for j in range(n_chunks):
        cur = nxt
        if j + 1 < n_chunks:
            nxt = up(j + 1)
        if j == n_chunks - 2:
            down = _dot(hid_ref[:, :split], wdn_ref[:split, :])
        act(j, *cur)
    y = x + (down + _dot(hid_ref[:, split:], wdn_ref[split:, :]))
    if final_norm:
        y = _rmsnorm(y, gf_ref[...])
    y_ref[...] = y


def _conv_ffn(x, g2, wup, cw, cb, wdn, gf, batch, seq, tile, final_norm):
    n = batch * seq
    nt = seq // tile
    hb = tile // HALO
    n_hb = n // HALO
    row = lambda b, t: (b * nt + t, 0)
    prev = lambda b, t: (jnp.maximum((b * nt + t) * hb - 1, 0), 0)
    nxt = lambda b, t: (jnp.minimum((b * nt + t + 1) * hb, n_hb - 1), 0)
    full = lambda b, t: (0, 0)
    in_specs = [
        pl.BlockSpec((tile, D_MODEL), row), pl.BlockSpec((HALO, D_MODEL), prev), pl.BlockSpec((HALO, D_MODEL), nxt),
        pl.BlockSpec((1, D_MODEL), full),
        pl.BlockSpec((D_MODEL, 2 * D_FF), full),
        pl.BlockSpec((3, D_FF), full), pl.BlockSpec((1, D_FF), full),
        pl.BlockSpec((D_FF, D_MODEL), full),
        pl.BlockSpec((1, D_MODEL), full),
    ]
    scratch = [pltpu.VMEM((tile + 2 * HALO, D_MODEL), BF16), pltpu.VMEM((tile, D_FF), BF16)]
    block_bytes = 2 * _nbytes((tile + HALO, D_MODEL), F32)
    scratch_bytes = (_nbytes((D_MODEL, 2 * D_FF), BF16) + _nbytes((D_FF, D_MODEL), BF16)
                     + _nbytes((tile + 2 * HALO, D_MODEL), BF16) + _nbytes((tile, D_FF), BF16))
    return pl.pallas_call(
        functools.partial(_conv_ffn_kernel, tile=tile, final_norm=final_norm),
        grid=(batch, nt),
        in_specs=in_specs,
        out_specs=pl.BlockSpec((tile, D_MODEL), row),
        out_shape=jax.ShapeDtypeStruct((n, D_MODEL), F32),
        scratch_shapes=scratch,
        compiler_params=pltpu.CompilerParams(
            dimension_semantics=("arbitrary", "arbitrary"),
            vmem_limit_bytes=_vmem_limit(block_bytes, scratch_bytes)),
        name="conv_ffn",
    )(x, x, x, g2, wup, cw, cb, wdn, gf)


def _tiles(seq):
    tok = min(512, seq)
    assert seq % tok == 0 and tok % GLA_BLOCK == 0 and tok % (2 * HALO * ROW_SUB) == 0
    return tok


def _prep_layer(l, w_in, w_lr2_f, b_lr_f, w_lr2_b, b_lr_b, w_pool_grp, w_br_a, w_br_b, w_out, w_up, w_down):
    lr0 = 2 * QK_W + 2 * V_W
    lr1 = lr0 + 2 * GLA_LOWRANK
    wl = w_in[l]
    wm = jnp.concatenate([wl[:, :lr0], wl[:, lr1:]], axis=1).astype(BF16)
    wlr = jnp.pad(wl[:, lr0:lr1], ((0, 0), (0, LR_PAD - 2 * GLA_LOWRANK))).astype(BF16)
    wlr2 = jnp.zeros((LR_PAD, 2 * QK_W), F32)
    wlr2 = wlr2.at[:GLA_LOWRANK, :QK_W].set(w_lr2_f[l])
    wlr2 = wlr2.at[GLA_LOWRANK:2 * GLA_LOWRANK, QK_W:].set(w_lr2_b[l]).astype(BF16)
    blr = jnp.concatenate([b_lr_f[l], b_lr_b[l]])[None, :]
    return dict(wm=wm, wlr=wlr, wlr2=wlr2, blr=blr, wgrp=w_pool_grp[l].astype(BF16),
                wa=w_br_a[l].astype(BF16), wb=w_br_b[l].astype(BF16), wo=w_out[l].astype(BF16),
                wup=w_up[l].astype(BF16), wdn=w_down[l].astype(BF16))


def _trunk(x, layers, norm1_g, onorm_g, pool_scale, norm2_g, conv_w, conv_b, final_g):
    batch, seq, _ = x.shape
    tile = _tiles(seq)
    depth = len(layers)
    x = x.reshape(batch * seq, D_MODEL)
    for l, w in enumerate(layers):
        q, k, v, sr, u, sga, sgb, g, g_min = _in_proj(x, norm1_g[l][None, :], w["wm"], w["wlr"], w["wlr2"],
                                                      w["blr"], tile)
        o_f, o_b = _gla(q, k, v, g, g_min[:, 0, 0], batch, seq, tile)
        x = _mix_out(o_f, o_b, sr, u, sga, sgb, x, onorm_g[l][None, :], w["wgrp"], pool_scale[l][None, :],
                     w["wa"], w["wb"], w["wo"], batch, seq, tile)
        x = _conv_ffn(x, norm2_g[l][None, :], w["wup"], conv_w[l], conv_b[l][None, :], w["wdn"],
                      final_g[None, :], batch, seq, tile, final_norm=(l == depth - 1))
    return x.reshape(batch, seq, D_MODEL)


def kernel(x_prompt, x_sample, norm1_g, w_in, w_lr2_f, b_lr_f, w_lr2_b, b_lr_b, onorm_g, w_pool_grp, pool_scale, w_br_a, w_br_b, w_out, norm2_g, w_up, conv_w, conv_b, w_down, final_g):
    depth = w_in.shape[0]
    layers = [_prep_layer(l, w_in, w_lr2_f, b_lr_f, w_lr2_b, b_lr_b, w_pool_grp, w_br_a, w_br_b, w_out,
                          w_up, w_down) for l in range(depth)]
    args = (layers, norm1_g, onorm_g, pool_scale, norm2_g, conv_w, conv_b, final_g)
    return (_trunk(x_prompt, *args), _trunk(x_sample, *args))
```

```python
import functools

import jax
import jax.numpy as jnp
from jax import lax
from jax.experimental import pallas as pl
from jax.experimental.pallas import tpu as pltpu

D_MODEL = 1024
GLA_HEADS = 4
GLA_DK = 128
GLA_DV = 256
QK_W = GLA_HEADS * GLA_DK
V_W = GLA_HEADS * GLA_DV
GLA_LOWRANK = 16
GLA_TAU = 16.0
GLA_BLOCK = 128
POOL_GROUPS = 4
POOL_WIDTH = 512
POOL_GC = 128
POOL_WINDOWS = (2, 4, 8, 16)
D_FF = 2816
EPS = 1e-6

V7X_VMEM_BYTES = 64 * 1024 * 1024
V7X_LANES = 128
V7X_MXU_DIM = 256
BF16_SUBLANE_TILE = 16
F32_SUBLANE_TILE = 8

LR_PAD = V7X_LANES
HALO = BF16_SUBLANE_TILE
MAIN_W = 2 * QK_W + 2 * V_W + POOL_WIDTH + 2 * D_MODEL
FF_CHUNK = V7X_MXU_DIM
SUB_ROWS = 256
GLA_UNROLL = 4
EXP_RANGE = 80.0

F32 = jnp.float32
BF16 = jnp.bfloat16


def _vmem_limit(block_bytes, scratch_bytes):
    want = 2 * block_bytes + scratch_bytes + 16 * 1024 * 1024
    return int(min(want, V7X_VMEM_BYTES - 8 * 1024 * 1024))


def _nbytes(shape, dtype):
    n = 1
    for s in shape:
        n *= s
    return n * jnp.dtype(dtype).itemsize


def _rmsnorm(x, g):
    return x * lax.rsqrt(jnp.mean(x * x, axis=-1, keepdims=True) + EPS) * g


def _dot(a, b):
    return jnp.dot(a, b, preferred_element_type=F32)


def _dot_nt(a, b):
    return lax.dot_general(a, b, (((1,), (1,)), ((), ())), preferred_element_type=F32)


def _in_proj_kernel(x_ref, g1_ref, wm_ref, wlr_ref, wlr2_ref, blr_ref,
                    q_ref, k_ref, v_ref, sr_ref, u_ref, sga_ref, sgb_ref, g_ref, gmin_ref, *, tile):
    sub = SUB_ROWS
    n_sub = tile // sub
    cw = 512
    g_min = None
    for i in range(n_sub):
        rows = slice(i * sub, (i + 1) * sub)
        xn = _rmsnorm(x_ref[rows, :], g1_ref[...]).astype(BF16)

        def proj(c0, w):
            return _dot(xn, wm_ref[:, c0:c0 + w])

        lr = _dot(xn, wlr_ref[...]).astype(BF16)
        q_ref[rows, :] = (proj(0, QK_W) * (GLA_DK ** -0.5)).astype(BF16)
        k_ref[rows, :] = proj(QK_W, QK_W).astype(BF16)
        c = 2 * QK_W
        for j in range(V_W // cw):
            v_ref[rows, j * cw:(j + 1) * cw] = proj(c + j * cw, cw).astype(BF16)
        c += V_W
        z = _dot(lr, wlr2_ref[...]) + blr_ref[...]
        g = (jnp.minimum(z, 0.0) - jnp.log(1.0 + jnp.exp(-jnp.abs(z)))) * (1.0 / GLA_TAU)
        g_ref[rows, :] = g
        m = jnp.min(g)
        g_min = m if g_min is None else jnp.minimum(g_min, m)
        for j in range(V_W // cw):
            r = proj(c + j * cw, cw)
            sr_ref[rows, j * cw:(j + 1) * cw] = (r * jax.nn.sigmoid(r)).astype(BF16)
        c += V_W
        u_ref[rows, :] = proj(c, POOL_WIDTH).astype(BF16)
        c += POOL_WIDTH
        for j in range(D_MODEL // cw):
            sga_ref[rows, j * cw:(j + 1) * cw] = jax.nn.sigmoid(proj(c + j * cw, cw)).astype(BF16)
        c += D_MODEL
        for j in range(D_MODEL // cw):
            sgb_ref[rows, j * cw:(j + 1) * cw] = jax.nn.sigmoid(proj(c + j * cw, cw)).astype(BF16)
    gmin_ref[...] = jnp.full(gmin_ref.shape, g_min, F32)


def _in_proj(x, g1, wm, wlr, wlr2, blr, tile):
    n = x.shape[0]
    grid = (n // tile,)
    row = lambda i: (i, 0)
    full = lambda i: (0, 0)
    outs = [(QK_W, BF16), (QK_W, BF16), (V_W, BF16), (V_W, BF16), (POOL_WIDTH, BF16),
            (D_MODEL, BF16), (D_MODEL, BF16), (2 * QK_W, F32)]
    in_specs = [
        pl.BlockSpec((tile, D_MODEL), row),
        pl.BlockSpec((1, D_MODEL), full),
        pl.BlockSpec((D_MODEL, MAIN_W), full),
        pl.BlockSpec((D_MODEL, LR_PAD), full),
        pl.BlockSpec((LR_PAD, 2 * QK_W), full),
        pl.BlockSpec((1, 2 * QK_W), full),
    ]
    gmin_block = (1, F32_SUBLANE_TILE, V7X_LANES)
    block_bytes = _nbytes((tile, D_MODEL), F32) + sum(_nbytes((tile, w), d) for w, d in outs)
    weight_bytes = _nbytes((D_MODEL, MAIN_W + LR_PAD), BF16) + _nbytes((LR_PAD, 2 * QK_W), BF16)
    return pl.pallas_call(
        functools.partial(_in_proj_kernel, tile=tile),
        grid=grid,
        in_specs=in_specs,
        out_specs=[pl.BlockSpec((tile, w), row) for w, _ in outs]
        + [pl.BlockSpec(gmin_block, lambda i: (i, 0, 0))],
        out_shape=[jax.ShapeDtypeStruct((n, w), d) for w, d in outs]
        + [jax.ShapeDtypeStruct((n // tile,) + gmin_block[1:], F32)],
        compiler_params=pltpu.CompilerParams(
            dimension_semantics=("arbitrary",),
            vmem_limit_bytes=_vmem_limit(block_bytes, weight_bytes)),
        name="in_proj",
    )(x, g1, wm, wlr, wlr2, blr)


def _cum_log_decay(g, cum):
    g_hi = g.astype(BF16)
    g_lo = (g - g_hi.astype(F32)).astype(BF16)
    return _dot(cum, g_hi) + _dot(cum, g_lo)


def _decay_columns(s_decay):
    return jnp.broadcast_to(s_decay, (GLA_BLOCK, QK_W)).T


def _scale_state(s, dec_t):
    return s * jnp.concatenate([dec_t] * (GLA_DV // GLA_BLOCK), axis=1)


def _gla_prep(r0, b, q_ref, k_ref, v_ref, edge_row):
    rows = slice(r0, r0 + GLA_BLOCK)
    s_decay = jnp.exp(b[edge_row:edge_row + 1, :])
    q_dec = (q_ref[rows, :].astype(F32) * jnp.exp(b)).astype(BF16)
    k_inv = k_ref[rows, :].astype(F32) * jnp.exp(-b)
    k_dec = (k_inv * s_decay).astype(BF16)
    return q_dec, k_inv.astype(BF16), k_dec.T, _decay_columns(s_decay), v_ref[rows, :]


def _gla_fast(fwd_refs, bwd_refs, of_ref, ob_ref, sf_ref, sb_ref, cums, masks, rows_f, rows_b):
    ks = [slice(h * GLA_DK, (h + 1) * GLA_DK) for h in range(GLA_HEADS)]
    vs = [slice(h * GLA_DV, (h + 1) * GLA_DV) for h in range(GLA_HEADS)]
    heads = [(d, h) for d in range(2) for h in range(GLA_HEADS)]
    o_refs = (of_ref, ob_ref)
    state = {(d, h): (sf_ref, sb_ref)[d][h] for d, h in heads}

    def outputs(blocks, scores, pre_state):
        for d, h in heads:
            r0, (q_dec, _, _, _, v) = blocks[d]
            a = jnp.where(masks[d], scores[d, h], 0.0).astype(BF16)
            lhs = jnp.concatenate([q_dec[:, ks[h]], a], axis=1)
            rhs = jnp.concatenate([pre_state[d, h].astype(BF16), v[:, vs[h]]], axis=0)
            o_refs[d][r0:r0 + GLA_BLOCK, vs[h]] = _dot(lhs, rhs).astype(o_refs[d].dtype)

    b_f = [_cum_log_decay(fwd_refs[3][r0:r0 + GLA_BLOCK, :], cums[0]) for r0 in rows_f]
    b_b = [_cum_log_decay(bwd_refs[3][r0:r0 + GLA_BLOCK, :], cums[1]) for r0 in rows_b]
    pending = None
    for step, (rf, rb) in enumerate(zip(rows_f, rows_b)):
        blocks = ((rf, _gla_prep(rf, b_f[step], *fwd_refs[:3], GLA_BLOCK - 1)),
                  (rb, _gla_prep(rb, b_b[step], *bwd_refs[:3], 0)))
        scores, upd = {}, {}
        for d, h in heads:
            q_dec, k_inv = blocks[d][1][0], blocks[d][1][1]
            scores[d, h] = _dot_nt(q_dec[:, ks[h]], k_inv[:, ks[h]])
        for d, h in heads:
            kd_t, v = blocks[d][1][2], blocks[d][1][4]
            upd[d, h] = _dot(kd_t[ks[h], :], v[:, vs[h]])
        if pending is not None:
            outputs(*pending)
        pending = (blocks, scores, dict(state))
        for d, h in heads:
            dec_t = blocks[d][1][3]
            state[d, h] = _scale_state(state[d, h], dec_t[ks[h], :]) + upd[d, h]
    outputs(*pending)
    for d, h in heads:
        (sf_ref, sb_ref)[d][h] = state[d, h]


def _gla_exact(refs, o_ref, s_ref, cum, mask, edge_row, reverse, n_blocks, b_scr, q_scr, k_scr, sc_scr):
    q_ref, k_ref, v_ref, g_ref = refs
    ks = [slice(h * GLA_DK, (h + 1) * GLA_DK) for h in range(GLA_HEADS)]
    vs = [slice(h * GLA_DV, (h + 1) * GLA_DV) for h in range(GLA_HEADS)]
    lane = lax.broadcasted_iota(jnp.int32, (GLA_BLOCK, GLA_BLOCK), 1)

    def block(c, carry):
        blk = (n_blocks - 1 - c) if reverse else c
        rows = pl.ds(pl.multiple_of(blk * GLA_BLOCK, GLA_BLOCK), GLA_BLOCK)
        b = _cum_log_decay(g_ref[rows, :], cum)
        q = q_ref[rows, :].astype(F32)
        k = k_ref[rows, :].astype(F32)
        b_scr[...] = b
        q_scr[...] = q
        k_scr[...] = k
        sc_scr[...] = jnp.zeros_like(sc_scr)

        def column(j, carry):
            w = jnp.exp(jnp.minimum(b_scr[...] - b_scr[pl.ds(j, 1), :], 0.0))
            contrib = q_scr[...] * w * k_scr[pl.ds(j, 1), :]
            for h in range(GLA_HEADS):
                col = jnp.sum(contrib[:, ks[h]], axis=1, keepdims=True)
                sc_scr[h] += jnp.where(lane == j, col, 0.0)
            return carry

        lax.fori_loop(0, GLA_BLOCK, column, 0)
        b_edge = b[edge_row:edge_row + 1, :]
        s_decay = jnp.exp(b_edge)
        q_dec = (q * jnp.exp(b)).astype(BF16)
        kd_t = (k * jnp.exp(b_edge - b)).astype(BF16).T
        dec_t = _decay_columns(s_decay)
        v = v_ref[rows, :]
        for h in range(GLA_HEADS):
            s = s_ref[h]
            a = jnp.where(mask, sc_scr[h], 0.0).astype(BF16)
            lhs = jnp.concatenate([q_dec[:, ks[h]], a], axis=1)
            rhs = jnp.concatenate([s.astype(BF16), v[:, vs[h]]], axis=0)
            o_ref[rows, vs[h]] = _dot(lhs, rhs).astype(o_ref.dtype)
            s_ref[h] = _scale_state(s, dec_t[ks[h], :]) + _dot(kd_t[ks[h], :], v[:, vs[h]])
        return carry

    lax.fori_loop(0, n_blocks, block, 0)


def _gla_kernel(gmin_ref, qf_ref, kf_ref, vf_ref, gf_ref, qb_ref, kb_ref, vb_ref, gb_ref,
                of_ref, ob_ref, sf_ref, sb_ref, b_scr, q_scr, k_scr, sc_scr, *, n_blocks):
    bi = pl.program_id(0)
    t = pl.program_id(1)
    nt = pl.num_programs(1)

    @pl.when(t == 0)
    def _():
        sf_ref[...] = jnp.zeros_like(sf_ref)
        sb_ref[...] = jnp.zeros_like(sb_ref)

    ri = lax.broadcasted_iota(jnp.int32, (GLA_BLOCK, GLA_BLOCK), 0)
    ci = lax.broadcasted_iota(jnp.int32, (GLA_BLOCK, GLA_BLOCK), 1)
    masks = (ri >= ci, ri <= ci)
    cums = (masks[0].astype(BF16), masks[1].astype(BF16))
    fwd_refs = (qf_ref, kf_ref, vf_ref, gf_ref)
    bwd_refs = (qb_ref, kb_ref, vb_ref, gb_ref)

    g_min = jnp.minimum(gmin_ref[bi * nt + t], gmin_ref[bi * nt + (nt - 1 - t)])
    factorisable = g_min * GLA_BLOCK > -EXP_RANGE

    @pl.when(factorisable)
    def _():
        offs = [c * GLA_BLOCK for c in range(n_blocks)]
        for p in range(0, n_blocks, GLA_UNROLL):
            _gla_fast(fwd_refs, bwd_refs, of_ref, ob_ref, sf_ref, sb_ref, cums, masks,
                      offs[p:p + GLA_UNROLL], offs[::-1][p:p + GLA_UNROLL])

    @pl.when(jnp.logical_not(factorisable))
    def _():
        scr = (b_scr, q_scr, k_scr, sc_scr)
        _gla_exact(fwd_refs, of_ref, sf_ref, cums[0], masks[0], GLA_BLOCK - 1, False, n_blocks, *scr)
        _gla_exact(bwd_refs, ob_ref, sb_ref, cums[1], masks[1], 0, True, n_blocks, *scr)


def _gla(q, k, v, g, g_min, batch, seq, tile):
    n = batch * seq
    nt = seq // tile
    fwd = lambda b, t, gm: (b * nt + t, 0)
    bwd = lambda b, t, gm: (b * nt + (nt - 1 - t), 0)
    bwd_g = lambda b, t, gm: (b * nt + (nt - 1 - t), 1)
    in_specs = [
        pl.BlockSpec((tile, QK_W), fwd), pl.BlockSpec((tile, QK_W), fwd),
        pl.BlockSpec((tile, V_W), fwd), pl.BlockSpec((tile, QK_W), fwd),
        pl.BlockSpec((tile, QK_W), bwd), pl.BlockSpec((tile, QK_W), bwd),
        pl.BlockSpec((tile, V_W), bwd), pl.BlockSpec((tile, QK_W), bwd_g),
    ]
    state = pltpu.VMEM((GLA_HEADS, GLA_DK, GLA_DV), F32)
    rows_f32 = pltpu.VMEM((GLA_BLOCK, QK_W), F32)
    scratch = [state, state, rows_f32, rows_f32, rows_f32, pltpu.VMEM((GLA_HEADS, GLA_BLOCK, GLA_BLOCK), F32)]
    block_bytes = 2 * (2 * _nbytes((tile, QK_W), BF16) + 2 * _nbytes((tile, V_W), BF16)
                       + _nbytes((tile, QK_W), F32))
    scratch_bytes = (2 * _nbytes((GLA_HEADS, GLA_DV, GLA_DK), F32) + 3 * _nbytes((GLA_BLOCK, QK_W), F32)
                     + _nbytes((GLA_HEADS, GLA_BLOCK, GLA_BLOCK), F32))
    grid_spec = pltpu.PrefetchScalarGridSpec(
        num_scalar_prefetch=1,
        grid=(batch, nt),
        in_specs=in_specs,
        out_specs=[pl.BlockSpec((tile, V_W), fwd), pl.BlockSpec((tile, V_W), bwd)],
        scratch_shapes=scratch,
    )
    return pl.pallas_call(
        functools.partial(_gla_kernel, n_blocks=tile // GLA_BLOCK),
        grid_spec=grid_spec,
        out_shape=[jax.ShapeDtypeStruct((n, V_W), BF16)] * 2,
        compiler_params=pltpu.CompilerParams(
            dimension_semantics=("arbitrary", "arbitrary"),
            vmem_limit_bytes=_vmem_limit(block_bytes, scratch_bytes)),
        name="gla",
    )(g_min, q, k, v, g, q, k, v, g)


def _mix_out_kernel(of_ref, ob_ref, sr_ref, u_ref, up_ref, un_ref, sga_ref, sgb_ref, x_ref,
                    og_ref, wgrp_ref, ps_ref, wa_ref, wb_ref, wo_ref, y_ref, *, tile, seq):
    t = pl.program_id(1)
    nt = pl.num_programs(1)
    sub = SUB_ROWS
    n_sub = tile // sub

    u_prev = jnp.where(t > 0, up_ref[...].astype(F32), 0.0)
    u_next = jnp.where(t < nt - 1, un_ref[...].astype(F32), 0.0)
    u_ext = jnp.concatenate([u_prev, u_ref[...].astype(F32), u_next], axis=0)

    def window_mean(centred, w, ahead, r0):
        def edge(e0):
            pos = t * tile + r0 + e0 + lax.broadcasted_iota(jnp.int32, (HALO, POOL_GC), 0)
            cnt = jnp.minimum(pos + ahead, seq - 1) - jnp.maximum(pos - w // 2, 0) + 1
            return centred[e0:e0 + HALO, :] / cnt.astype(F32)
        inner = centred[HALO:sub - HALO, :] * (1.0 / w)
        return jnp.concatenate([edge(0), inner, edge(sub - HALO)], axis=0)

    def branches(i):
        r0 = i * sub
        rows = slice(r0, r0 + sub)
        o = of_ref[rows, :].astype(F32) + ob_ref[rows, :].astype(F32)
        normed = []
        for h in range(GLA_HEADS):
            oh = o[:, h * GLA_DV:(h + 1) * GLA_DV]
            normed.append(oh * lax.rsqrt(jnp.mean(oh * oh, axis=-1, keepdims=True) + EPS))
        a_in = (jnp.concatenate(normed, axis=-1) * og_ref[...]).astype(BF16) * sr_ref[rows, :]
        a_out = _dot(a_in, wa_ref[...])
        ext = sub + 2 * HALO
        pooled = []
        for gi, w in enumerate(POOL_WINDOWS):
            ug = u_ext[r0:r0 + ext, gi * POOL_GC:(gi + 1) * POOL_GC]
            trail = ug
            span = 1
            while span < w:
                trail = trail + pltpu.roll(trail, span, 0)
                span *= 2
            ahead = w - 1 - w // 2
            centred = pltpu.roll(trail, ext - ahead, 0) if ahead else trail
            p = window_mean(centred[HALO:HALO + sub, :], w, ahead, r0) - ug[HALO:HALO + sub, :]
            pooled.append(_dot(p.astype(BF16), wgrp_ref[gi]))
        pb = (jnp.concatenate(pooled, axis=-1) * ps_ref[...]).astype(BF16)
        b_out = _dot(pb, wb_ref[...])
        return a_out, b_out

    def merge(i, a_out, b_out):
        rows = slice(i * sub, (i + 1) * sub)
        merged = sga_ref[rows, :] * a_out.astype(BF16) + sgb_ref[rows, :] * b_out.astype(BF16)
        y_ref[rows, :] = x_ref[rows, :] + _dot(merged, wo_ref[...])

    pending = branches(0)
    for i in range(1, n_sub):
        nxt = branches(i)
        merge(i - 1, *pending)
        pending = nxt
    merge(n_sub - 1, *pending)


def _mix_out(o_f, o_b, sr, u, sga, sgb, x, og, wgrp, ps, wa, wb, wo, batch, seq, tile):
    n = batch * seq
    nt = seq // tile
    hb = tile // HALO
    n_hb = n // HALO
    row = lambda b, t: (b * nt + t, 0)
    prev = lambda b, t: (jnp.maximum((b * nt + t) * hb - 1, 0), 0)
    nxt = lambda b, t: (jnp.minimum((b * nt + t + 1) * hb, n_hb - 1), 0)
    full2 = lambda b, t: (0, 0)
    full3 = lambda b, t: (0, 0, 0)
    in_specs = [
        pl.BlockSpec((tile, V_W), row), pl.BlockSpec((tile, V_W), row), pl.BlockSpec((tile, V_W), row),
        pl.BlockSpec((tile, POOL_WIDTH), row), pl.BlockSpec((HALO, POOL_WIDTH), prev),
        pl.BlockSpec((HALO, POOL_WIDTH), nxt),
        pl.BlockSpec((tile, D_MODEL), row), pl.BlockSpec((tile, D_MODEL), row), pl.BlockSpec((tile, D_MODEL), row),
        pl.BlockSpec((1, V_W), full2), pl.BlockSpec((POOL_GROUPS, POOL_GC, POOL_GC), full3),
        pl.BlockSpec((1, POOL_WIDTH), full2),
        pl.BlockSpec((V_W, D_MODEL), full2), pl.BlockSpec((POOL_WIDTH, D_MODEL), full2),
        pl.BlockSpec((D_MODEL, D_MODEL), full2),
    ]
    block_bytes = (5 * _nbytes((tile, D_MODEL), BF16) + _nbytes((tile + 2 * HALO, POOL_WIDTH), BF16)
                   + 2 * _nbytes((tile, D_MODEL), F32))
    weight_bytes = (_nbytes((V_W + POOL_WIDTH + D_MODEL, D_MODEL), BF16)
                    + _nbytes((POOL_GROUPS, POOL_GC, POOL_GC), BF16))
    return pl.pallas_call(
        functools.partial(_mix_out_kernel, tile=tile, seq=seq),
        grid=(batch, nt),
        in_specs=in_specs,
        out_specs=pl.BlockSpec((tile, D_MODEL), row),
        out_shape=jax.ShapeDtypeStruct((n, D_MODEL), F32),
        compiler_params=pltpu.CompilerParams(
            dimension_semantics=("arbitrary", "arbitrary"),
            vmem_limit_bytes=_vmem_limit(block_bytes, weight_bytes)),
        name="mix_out",
    )(o_f, o_b, sr, u, u, u, sga, sgb, x, og, wgrp, ps, wa, wb, wo)


def _conv_ffn_kernel(x_ref, xp_ref, xn_ref, g2_ref, wup_ref, cw_ref, cb_ref, wdn_ref, gf_ref,
                     y_ref, xe_ref, hid_ref, *, tile, final_norm):
    t = pl.program_id(1)
    nt = pl.num_programs(1)
    g2 = g2_ref[...]
    x = x_ref[...]
    xe_ref[0:HALO, :] = jnp.where(t > 0, _rmsnorm(xp_ref[...], g2), 0.0).astype(BF16)
    xe_ref[HALO:HALO + tile, :] = _rmsnorm(x, g2).astype(BF16)
    xe_ref[HALO + tile:, :] = jnp.where(t < nt - 1, _rmsnorm(xn_ref[...], g2), 0.0).astype(BF16)
    ext = tile + 2 * HALO
    n_chunks = D_FF // FF_CHUNK

    def up(j):
        a = _dot(xe_ref[...], wup_ref[:, j * FF_CHUNK:(j + 1) * FF_CHUNK])
        val = _dot(xe_ref[HALO:HALO + tile, :], wup_ref[:, D_FF + j * FF_CHUNK:D_FF + (j + 1) * FF_CHUNK])
        return a, val

    def act(j, a, val):
        cs = slice(j * FF_CHUNK, (j + 1) * FF_CHUNK)
        cw = cw_ref[:, cs]
        conv = (pltpu.roll(a, 1, 0) * cw[0:1, :] + a * cw[1:2, :]
                + pltpu.roll(a, ext - 1, 0) * cw[2:3, :])[HALO:HALO + tile, :] + cb_ref[:, cs]
        hid_ref[:, cs] = (conv * jax.nn.sigmoid(conv) * val).astype(BF16)

    split = (n_chunks - 2) * FF_CHUNK
    nxt = up(0)
    down = None
    for j in range(n_chunks):
        cur = nxt
        if j + 1 < n_chunks:
            nxt = up(j + 1)
        if j == n_chunks - 2:
            down = _dot(hid_ref[:, :split], wdn_ref[:split, :])
        act(j, *cur)
    y = x + (down + _dot(hid_ref[:, split:], wdn_ref[split:, :]))
    if final_norm:
        y = _rmsnorm(y, gf_ref[...])
    y_ref[...] = y


def _conv_ffn(x, g2, wup, cw, cb, wdn, gf, batch, seq, tile, final_norm):
    n = batch * seq
    nt = seq // tile
    hb = tile // HALO
    n_hb = n // HALO
    row = lambda b, t: (b * nt + t, 0)
    prev = lambda b, t: (jnp.maximum((b * nt + t) * hb - 1, 0), 0)
    nxt = lambda b, t: (jnp.minimum((b * nt + t + 1) * hb, n_hb - 1), 0)
    full = lambda b, t: (0, 0)
    in_specs = [
        pl.BlockSpec((tile, D_MODEL), row), pl.BlockSpec((HALO, D_MODEL), prev), pl.BlockSpec((HALO, D_MODEL), nxt),
        pl.BlockSpec((1, D_MODEL), full),
        pl.BlockSpec((D_MODEL, 2 * D_FF), full),
        pl.BlockSpec((3, D_FF), full), pl.BlockSpec((1, D_FF), full),
        pl.BlockSpec((D_FF, D_MODEL), full),
        pl.BlockSpec((1, D_MODEL), full),
    ]
    scratch = [pltpu.VMEM((tile + 2 * HALO, D_MODEL), BF16), pltpu.VMEM((tile, D_FF), BF16)]
    block_bytes = 2 * _nbytes((tile + HALO, D_MODEL), F32)
    scratch_bytes = (_nbytes((D_MODEL, 2 * D_FF), BF16) + _nbytes((D_FF, D_MODEL), BF16)
                     + _nbytes((tile + 2 * HALO, D_MODEL), BF16) + _nbytes((tile, D_FF), BF16))
    return pl.pallas_call(
        functools.partial(_conv_ffn_kernel, tile=tile, final_norm=final_norm),
        grid=(batch, nt),
        in_specs=in_specs,
        out_specs=pl.BlockSpec((tile, D_MODEL), row),
        out_shape=jax.ShapeDtypeStruct((n, D_MODEL), F32),
        scratch_shapes=scratch,
        compiler_params=pltpu.CompilerParams(
            dimension_semantics=("arbitrary", "arbitrary"),
            vmem_limit_bytes=_vmem_limit(block_bytes, scratch_bytes)),
        name="conv_ffn",
    )(x, x, x, g2, wup, cw, cb, wdn, gf)


def _tiles(seq):
    small, big = min(512, seq), min(1024, seq)
    for tok in (small, big):
        assert seq % tok == 0 and tok % (GLA_BLOCK * GLA_UNROLL) == 0 and tok % SUB_ROWS == 0
    return small, big


def _prep_layer(l, w_in, w_lr2_f, b_lr_f, w_lr2_b, b_lr_b, w_pool_grp, w_br_a, w_br_b, w_out, w_up, w_down):
    lr0 = 2 * QK_W + 2 * V_W
    lr1 = lr0 + 2 * GLA_LOWRANK
    wl = w_in[l]
    wm = jnp.concatenate([wl[:, :lr0], wl[:, lr1:]], axis=1).astype(BF16)
    wlr = jnp.pad(wl[:, lr0:lr1], ((0, 0), (0, LR_PAD - 2 * GLA_LOWRANK))).astype(BF16)
    wlr2 = jnp.zeros((LR_PAD, 2 * QK_W), F32)
    wlr2 = wlr2.at[:GLA_LOWRANK, :QK_W].set(w_lr2_f[l])
    wlr2 = wlr2.at[GLA_LOWRANK:2 * GLA_LOWRANK, QK_W:].set(w_lr2_b[l]).astype(BF16)
    blr = jnp.concatenate([b_lr_f[l], b_lr_b[l]])[None, :]
    return dict(wm=wm, wlr=wlr, wlr2=wlr2, blr=blr, wgrp=w_pool_grp[l].astype(BF16),
                wa=w_br_a[l].astype(BF16), wb=w_br_b[l].astype(BF16), wo=w_out[l].astype(BF16),
                wup=w_up[l].astype(BF16), wdn=w_down[l].astype(BF16))


def _trunk(x, layers, norm1_g, onorm_g, pool_scale, norm2_g, conv_w, conv_b, final_g):
    batch, seq, _ = x.shape
    tile_in, tile = _tiles(seq)
    depth = len(layers)
    x = x.reshape(batch * seq, D_MODEL)
    for l, w in enumerate(layers):
        q, k, v, sr, u, sga, sgb, g, g_min = _in_proj(x, norm1_g[l][None, :], w["wm"], w["wlr"], w["wlr2"],
                                                      w["blr"], tile_in)
        g_min = jnp.min(g_min[:, 0, 0].reshape(-1, tile // tile_in), axis=1)
        o_f, o_b = _gla(q, k, v, g, g_min, batch, seq, tile)
        x = _mix_out(o_f, o_b, sr, u, sga, sgb, x, onorm_g[l][None, :], w["wgrp"], pool_scale[l][None, :],
                     w["wa"], w["wb"], w["wo"], batch, seq, tile)
        x = _conv_ffn(x, norm2_g[l][None, :], w["wup"], conv_w[l], conv_b[l][None, :], w["wdn"],
                      final_g[None, :], batch, seq, tile, final_norm=(l == depth - 1))
    return x.reshape(batch, seq, D_MODEL)


def kernel(x_prompt, x_sample, norm1_g, w_in, w_lr2_f, b_lr_f, w_lr2_b, b_lr_b, onorm_g, w_pool_grp, pool_scale, w_br_a, w_br_b, w_out, norm2_g, w_up, conv_w, conv_b, w_down, final_g):
    depth = w_in.shape[0]
    layers = [_prep_layer(l, w_in, w_lr2_f, b_lr_f, w_lr2_b, b_lr_b, w_pool_grp, w_br_a, w_br_b, w_out,
                          w_up, w_down) for l in range(depth)]
    args = (layers, norm1_g, onorm_g, pool_scale, norm2_g, conv_w, conv_b, final_g)
    return (_trunk(x_prompt, *args), _trunk(x_sample, *args))
```

```python
import functools

import jax
import jax.numpy as jnp
from jax import lax
from jax.experimental import pallas as pl
from jax.experimental.pallas import tpu as pltpu

D_MODEL = 1024
GLA_HEADS = 4
GLA_DK = 128
GLA_DV = 256
QK_W = GLA_HEADS * GLA_DK
V_W = GLA_HEADS * GLA_DV
GLA_LOWRANK = 16
GLA_TAU = 16.0
GLA_BLOCK = 128
POOL_GROUPS = 4
POOL_WIDTH = 512
POOL_GC = 128
POOL_WINDOWS = (2, 4, 8, 16)
D_FF = 2816
EPS = 1e-6

V7X_VMEM_BYTES = 64 * 1024 * 1024
V7X_LANES = 128
V7X_MXU_DIM = 256
BF16_SUBLANE_TILE = 16
F32_SUBLANE_TILE = 8

LR_PAD = V7X_LANES
HALO = BF16_SUBLANE_TILE
MAIN_W = 2 * QK_W + 2 * V_W + POOL_WIDTH + 2 * D_MODEL
FF_CHUNK = V7X_MXU_DIM
SUB_ROWS = 256
GLA_AHEAD = 4
EXP_RANGE = 80.0

F32 = jnp.float32
BF16 = jnp.bfloat16


def _vmem_limit(block_bytes, scratch_bytes):
    want = 2 * block_bytes + scratch_bytes + 16 * 1024 * 1024
    return int(min(want, V7X_VMEM_BYTES - 8 * 1024 * 1024))


def _nbytes(shape, dtype):
    n = 1
    for s in shape:
        n *= s
    return n * jnp.dtype(dtype).itemsize


def _rmsnorm(x, g):
    return x * lax.rsqrt(jnp.mean(x * x, axis=-1, keepdims=True) + EPS) * g


def _dot(a, b):
    return jnp.dot(a, b, preferred_element_type=F32)


def _dot_nt(a, b):
    return lax.dot_general(a, b, (((1,), (1,)), ((), ())), preferred_element_type=F32)


def _layer_spec(shape, l):
    zeros = (0,) * len(shape)
    return pl.BlockSpec((None,) + tuple(shape), lambda *_: (l,) + zeros, pipeline_mode=pl.Buffered(1))


def _in_proj_kernel(x_ref, g1_ref, wm_ref, wlr_ref, wlr2_ref, blr_ref,
                    q_ref, k_ref, v_ref, sr_ref, u_ref, sga_ref, sgb_ref, g_ref, gmin_ref, *, tile):
    sub = SUB_ROWS
    n_sub = tile // sub
    cw = 512
    g_min = None
    for i in range(n_sub):
        rows = slice(i * sub, (i + 1) * sub)
        xn = _rmsnorm(x_ref[rows, :], g1_ref[...]).astype(BF16)

        def proj(c0, w):
            return _dot(xn, wm_ref[:, c0:c0 + w])

        lr = _dot(xn, wlr_ref[...]).astype(BF16)
        q_ref[rows, :] = (proj(0, QK_W) * (GLA_DK ** -0.5)).astype(BF16)
        k_ref[rows, :] = proj(QK_W, QK_W).astype(BF16)
        c = 2 * QK_W
        for j in range(V_W // cw):
            v_ref[rows, j * cw:(j + 1) * cw] = proj(c + j * cw, cw).astype(BF16)
        c += V_W
        z = _dot(lr, wlr2_ref[...]) + blr_ref[...]
        g = (jnp.minimum(z, 0.0) - jnp.log(1.0 + jnp.exp(-jnp.abs(z)))) * (1.0 / GLA_TAU)
        g_ref[rows, :] = g
        m = jnp.min(g)
        g_min = m if g_min is None else jnp.minimum(g_min, m)
        for j in range(V_W // cw):
            r = proj(c + j * cw, cw)
            sr_ref[rows, j * cw:(j + 1) * cw] = (r * jax.nn.sigmoid(r)).astype(BF16)
        c += V_W
        u_ref[rows, :] = proj(c, POOL_WIDTH).astype(BF16)
        c += POOL_WIDTH
        for j in range(D_MODEL // cw):
            sga_ref[rows, j * cw:(j + 1) * cw] = jax.nn.sigmoid(proj(c + j * cw, cw)).astype(BF16)
        c += D_MODEL
        for j in range(D_MODEL // cw):
            sgb_ref[rows, j * cw:(j + 1) * cw] = jax.nn.sigmoid(proj(c + j * cw, cw)).astype(BF16)
    gmin_ref[...] = jnp.full(gmin_ref.shape, g_min, F32)


def _in_proj(x, g1, wm, wlr, wlr2, blr, l, tile):
    n = x.shape[0]
    grid = (n // tile,)
    row = lambda i: (i, 0)
    outs = [(QK_W, BF16), (QK_W, BF16), (V_W, BF16), (V_W, BF16), (POOL_WIDTH, BF16),
            (D_MODEL, BF16), (D_MODEL, BF16), (2 * QK_W, F32)]
    in_specs = [
        pl.BlockSpec((tile, D_MODEL), row),
        _layer_spec((1, D_MODEL), l),
        _layer_spec((D_MODEL, MAIN_W), l),
        _layer_spec((D_MODEL, LR_PAD), l),
        _layer_spec((LR_PAD, 2 * QK_W), l),
        _layer_spec((1, 2 * QK_W), l),
    ]
    gmin_block = (1, F32_SUBLANE_TILE, V7X_LANES)
    block_bytes = _nbytes((tile, D_MODEL), F32) + sum(_nbytes((tile, w), d) for w, d in outs)
    weight_bytes = _nbytes((D_MODEL, MAIN_W + LR_PAD), BF16) + _nbytes((LR_PAD, 2 * QK_W), BF16)
    return pl.pallas_call(
        functools.partial(_in_proj_kernel, tile=tile),
        grid=grid,
        in_specs=in_specs,
        out_specs=[pl.BlockSpec((tile, w), row) for w, _ in outs]
        + [pl.BlockSpec(gmin_block, lambda i: (i, 0, 0))],
        out_shape=[jax.ShapeDtypeStruct((n, w), d) for w, d in outs]
        + [jax.ShapeDtypeStruct((n // tile,) + gmin_block[1:], F32)],
        compiler_params=pltpu.CompilerParams(
            dimension_semantics=("arbitrary",),
            vmem_limit_bytes=_vmem_limit(block_bytes, weight_bytes)),
        name="in_proj",
    )(x, g1, wm, wlr, wlr2, blr)


def _cum_log_decay(g, cum):
    g_hi = g.astype(BF16)
    g_lo = (g - g_hi.astype(F32)).astype(BF16)
    return _dot(cum, g_hi) + _dot(cum, g_lo)


def _decay_columns(s_decay):
    return jnp.broadcast_to(s_decay, (GLA_BLOCK, QK_W)).T


def _scale_state(s, dec_t):
    return s * jnp.concatenate([dec_t] * (GLA_DV // GLA_BLOCK), axis=1)


def _gla_prep(r0, b, q_ref, k_ref, v_ref, edge_row):
    rows = slice(r0, r0 + GLA_BLOCK)
    s_decay = jnp.exp(b[edge_row:edge_row + 1, :])
    q_dec = (q_ref[rows, :].astype(F32) * jnp.exp(b)).astype(BF16)
    k_inv = k_ref[rows, :].astype(F32) * jnp.exp(-b)
    k_dec = (k_inv * s_decay).astype(BF16)
    return q_dec, k_inv.astype(BF16), k_dec.T, _decay_columns(s_decay), v_ref[rows, :]


def _gla_fast(fwd_refs, bwd_refs, of_ref, ob_ref, sf_ref, sb_ref, cums, masks, rows_f, rows_b):
    ks = [slice(h * GLA_DK, (h + 1) * GLA_DK) for h in range(GLA_HEADS)]
    vs = [slice(h * GLA_DV, (h + 1) * GLA_DV) for h in range(GLA_HEADS)]
    heads = [(d, h) for d in range(2) for h in range(GLA_HEADS)]
    o_refs = (of_ref, ob_ref)
    state = {(d, h): (sf_ref, sb_ref)[d][h] for d, h in heads}

    def outputs(blocks, scores, pre_state):
        for d, h in heads:
            r0, (q_dec, _, _, _, v) = blocks[d]
            a = jnp.where(masks[d], scores[d, h], 0.0).astype(BF16)
            lhs = jnp.concatenate([q_dec[:, ks[h]], a], axis=1)
            rhs = jnp.concatenate([pre_state[d, h].astype(BF16), v[:, vs[h]]], axis=0)
            o_refs[d][r0:r0 + GLA_BLOCK, vs[h]] = _dot(lhs, rhs).astype(o_refs[d].dtype)

    n = len(rows_f)
    b_f, b_b = {}, {}

    def cum_decays(steps):
        for s in steps:
            b_f[s] = _cum_log_decay(fwd_refs[3][rows_f[s]:rows_f[s] + GLA_BLOCK, :], cums[0])
            b_b[s] = _cum_log_decay(bwd_refs[3][rows_b[s]:rows_b[s] + GLA_BLOCK, :], cums[1])

    cum_decays(range(min(GLA_AHEAD, n)))
    pending = None
    for step, (rf, rb) in enumerate(zip(rows_f, rows_b)):
        if step % GLA_AHEAD == GLA_AHEAD - 2 and step + 2 < n:
            cum_decays(range(step + 2, min(step + 2 + GLA_AHEAD, n)))
        blocks = ((rf, _gla_prep(rf, b_f[step], *fwd_refs[:3], GLA_BLOCK - 1)),
                  (rb, _gla_prep(rb, b_b[step], *bwd_refs[:3], 0)))
        scores, upd = {}, {}
        for d, h in heads:
            q_dec, k_inv = blocks[d][1][0], blocks[d][1][1]
            scores[d, h] = _dot_nt(q_dec[:, ks[h]], k_inv[:, ks[h]])
        for d, h in heads:
            kd_t, v = blocks[d][1][2], blocks[d][1][4]
            upd[d, h] = _dot(kd_t[ks[h], :], v[:, vs[h]])
        if pending is not None:
            outputs(*pending)
        pending = (blocks, scores, dict(state))
        for d, h in heads:
            dec_t = blocks[d][1][3]
            state[d, h] = _scale_state(state[d, h], dec_t[ks[h], :]) + upd[d, h]
    outputs(*pending)
    for d, h in heads:
        (sf_ref, sb_ref)[d][h] = state[d, h]


def _gla_exact(refs, o_ref, s_ref, cum, mask, edge_row, reverse, n_blocks, b_scr, q_scr, k_scr, sc_scr):
    q_ref, k_ref, v_ref, g_ref = refs
    ks = [slice(h * GLA_DK, (h + 1) * GLA_DK) for h in range(GLA_HEADS)]
    vs = [slice(h * GLA_DV, (h + 1) * GLA_DV) for h in range(GLA_HEADS)]
    lane = lax.broadcasted_iota(jnp.int32, (GLA_BLOCK, GLA_BLOCK), 1)

    def block(c, carry):
        blk = (n_blocks - 1 - c) if reverse else c
        rows = pl.ds(pl.multiple_of(blk * GLA_BLOCK, GLA_BLOCK), GLA_BLOCK)
        b = _cum_log_decay(g_ref[rows, :], cum)
        q = q_ref[rows, :].astype(F32)
        k = k_ref[rows, :].astype(F32)
        b_scr[...] = b
        q_scr[...] = q
        k_scr[...] = k
        sc_scr[...] = jnp.zeros_like(sc_scr)

        def column(j, carry):
            w = jnp.exp(jnp.minimum(b_scr[...] - b_scr[pl.ds(j, 1), :], 0.0))
            contrib = q_scr[...] * w * k_scr[pl.ds(j, 1), :]
            for h in range(GLA_HEADS):
                col = jnp.sum(contrib[:, ks[h]], axis=1, keepdims=True)
                sc_scr[h] += jnp.where(lane == j, col, 0.0)
            return carry

        lax.fori_loop(0, GLA_BLOCK, column, 0)
        b_edge = b[edge_row:edge_row + 1, :]
        s_decay = jnp.exp(b_edge)
        q_dec = (q * jnp.exp(b)).astype(BF16)
        kd_t = (k * jnp.exp(b_edge - b)).astype(BF16).T
        dec_t = _decay_columns(s_decay)
        v = v_ref[rows, :]
        for h in range(GLA_HEADS):
            s = s_ref[h]
            a = jnp.where(mask, sc_scr[h], 0.0).astype(BF16)
            lhs = jnp.concatenate([q_dec[:, ks[h]], a], axis=1)
            rhs = jnp.concatenate([s.astype(BF16), v[:, vs[h]]], axis=0)
            o_ref[rows, vs[h]] = _dot(lhs, rhs).astype(o_ref.dtype)
            s_ref[h] = _scale_state(s, dec_t[ks[h], :]) + _dot(kd_t[ks[h], :], v[:, vs[h]])
        return carry

    lax.fori_loop(0, n_blocks, block, 0)


def _gla_kernel(gmin_ref, qf_ref, kf_ref, vf_ref, gf_ref, qb_ref, kb_ref, vb_ref, gb_ref,
                of_ref, ob_ref, sf_ref, sb_ref, b_scr, q_scr, k_scr, sc_scr, *, n_blocks):
    bi = pl.program_id(0)
    t = pl.program_id(1)
    nt = pl.num_programs(1)

    @pl.when(t == 0)
    def _():
        sf_ref[...] = jnp.zeros_like(sf_ref)
        sb_ref[...] = jnp.zeros_like(sb_ref)

    ri = lax.broadcasted_iota(jnp.int32, (GLA_BLOCK, GLA_BLOCK), 0)
    ci = lax.broadcasted_iota(jnp.int32, (GLA_BLOCK, GLA_BLOCK), 1)
    masks = (ri >= ci, ri <= ci)
    cums = (masks[0].astype(BF16), masks[1].astype(BF16))
    fwd_refs = (qf_ref, kf_ref, vf_ref, gf_ref)
    bwd_refs = (qb_ref, kb_ref, vb_ref, gb_ref)

    g_min = jnp.minimum(gmin_ref[bi * nt + t], gmin_ref[bi * nt + (nt - 1 - t)])
    factorisable = g_min * GLA_BLOCK > -EXP_RANGE

    @pl.when(factorisable)
    def _():
        offs = [c * GLA_BLOCK for c in range(n_blocks)]
        _gla_fast(fwd_refs, bwd_refs, of_ref, ob_ref, sf_ref, sb_ref, cums, masks, offs, offs[::-1])

    @pl.when(jnp.logical_not(factorisable))
    def _():
        scr = (b_scr, q_scr, k_scr, sc_scr)
        _gla_exact(fwd_refs, of_ref, sf_ref, cums[0], masks[0], GLA_BLOCK - 1, False, n_blocks, *scr)
        _gla_exact(bwd_refs, ob_ref, sb_ref, cums[1], masks[1], 0, True, n_blocks, *scr)


def _gla(q, k, v, g, g_min, batch, seq, tile):
    n = batch * seq
    nt = seq // tile
    fwd = lambda b, t, gm: (b * nt + t, 0)
    bwd = lambda b, t, gm: (b * nt + (nt - 1 - t), 0)
    bwd_g = lambda b, t, gm: (b * nt + (nt - 1 - t), 1)
    in_specs = [
        pl.BlockSpec((tile, QK_W), fwd), pl.BlockSpec((tile, QK_W), fwd),
        pl.BlockSpec((tile, V_W), fwd), pl.BlockSpec((tile, QK_W), fwd),
        pl.BlockSpec((tile, QK_W), bwd), pl.BlockSpec((tile, QK_W), bwd),
        pl.BlockSpec((tile, V_W), bwd), pl.BlockSpec((tile, QK_W), bwd_g),
    ]
    state = pltpu.VMEM((GLA_HEADS, GLA_DK, GLA_DV), F32)
    rows_f32 = pltpu.VMEM((GLA_BLOCK, QK_W), F32)
    scratch = [state, state, rows_f32, rows_f32, rows_f32, pltpu.VMEM((GLA_HEADS, GLA_BLOCK, GLA_BLOCK), F32)]
    block_bytes = 2 * (2 * _nbytes((tile, QK_W), BF16) + 2 * _nbytes((tile, V_W), BF16)
                       + _nbytes((tile, QK_W), F32))
    scratch_bytes = (2 * _nbytes((GLA_HEADS, GLA_DV, GLA_DK), F32) + 3 * _nbytes((GLA_BLOCK, QK_W), F32)
                     + _nbytes((GLA_HEADS, GLA_BLOCK, GLA_BLOCK), F32))
    grid_spec = pltpu.PrefetchScalarGridSpec(
        num_scalar_prefetch=1,
        grid=(batch, nt),
        in_specs=in_specs,
        out_specs=[pl.BlockSpec((tile, V_W), fwd), pl.BlockSpec((tile, V_W), bwd)],
        scratch_shapes=scratch,
    )
    return pl.pallas_call(
        functools.partial(_gla_kernel, n_blocks=tile // GLA_BLOCK),
        grid_spec=grid_spec,
        out_shape=[jax.ShapeDtypeStruct((n, V_W), BF16)] * 2,
        compiler_params=pltpu.CompilerParams(
            dimension_semantics=("arbitrary", "arbitrary"),
            vmem_limit_bytes=_vmem_limit(block_bytes, scratch_bytes)),
        name="gla",
    )(g_min, q, k, v, g, q, k, v, g)


def _mix_out_kernel(of_ref, ob_ref, sr_ref, u_ref, up_ref, un_ref, sga_ref, sgb_ref, x_ref,
                    og_ref, wgrp_ref, ps_ref, wa_ref, wb_ref, wo_ref, y_ref, *, tile, seq):
    t = pl.program_id(1)
    nt = pl.num_programs(1)
    sub = SUB_ROWS
    n_sub = tile // sub

    u_prev = jnp.where(t > 0, up_ref[...].astype(F32), 0.0)
    u_next = jnp.where(t < nt - 1, un_ref[...].astype(F32), 0.0)
    u_ext = jnp.concatenate([u_prev, u_ref[...].astype(F32), u_next], axis=0)

    def window_mean(centred, w, ahead, r0):
        def edge(e0):
            pos = t * tile + r0 + e0 + lax.broadcasted_iota(jnp.int32, (HALO, POOL_GC), 0)
            cnt = jnp.minimum(pos + ahead, seq - 1) - jnp.maximum(pos - w // 2, 0) + 1
            return centred[e0:e0 + HALO, :] / cnt.astype(F32)
        inner = centred[HALO:sub - HALO, :] * (1.0 / w)
        return jnp.concatenate([edge(0), inner, edge(sub - HALO)], axis=0)

    def branches(i):
        r0 = i * sub
        rows = slice(r0, r0 + sub)
        o = of_ref[rows, :].astype(F32) + ob_ref[rows, :].astype(F32)
        normed = []
        for h in range(GLA_HEADS):
            oh = o[:, h * GLA_DV:(h + 1) * GLA_DV]
            normed.append(oh * lax.rsqrt(jnp.mean(oh * oh, axis=-1, keepdims=True) + EPS))
        a_in = (jnp.concatenate(normed, axis=-1) * og_ref[...]).astype(BF16) * sr_ref[rows, :]
        a_out = _dot(a_in, wa_ref[...])
        ext = sub + 2 * HALO
        pooled = []
        for gi, w in enumerate(POOL_WINDOWS):
            ug = u_ext[r0:r0 + ext, gi * POOL_GC:(gi + 1) * POOL_GC]
            trail = ug
            span = 1
            while span < w:
                trail = trail + pltpu.roll(trail, span, 0)
                span *= 2
            ahead = w - 1 - w // 2
            centred = pltpu.roll(trail, ext - ahead, 0) if ahead else trail
            p = window_mean(centred[HALO:HALO + sub, :], w, ahead, r0) - ug[HALO:HALO + sub, :]
            pooled.append(_dot(p.astype(BF16), wgrp_ref[gi]))
        pb = (jnp.concatenate(pooled, axis=-1) * ps_ref[...]).astype(BF16)
        b_out = _dot(pb, wb_ref[...])
        return a_out, b_out

    def merge(i, a_out, b_out):
        rows = slice(i * sub, (i + 1) * sub)
        merged = sga_ref[rows, :] * a_out.astype(BF16) + sgb_ref[rows, :] * b_out.astype(BF16)
        y_ref[rows, :] = x_ref[rows, :] + _dot(merged, wo_ref[...])

    pending = branches(0)
    for i in range(1, n_sub):
        nxt = branches(i)
        merge(i - 1, *pending)
        pending = nxt
    merge(n_sub - 1, *pending)


def _mix_out(o_f, o_b, sr, u, sga, sgb, x, og, wgrp, ps, wa, wb, wo, l, batch, seq, tile):
    n = batch * seq
    nt = seq // tile
    hb = tile // HALO
    n_hb = n // HALO
    row = lambda b, t: (b * nt + t, 0)
    prev = lambda b, t: (jnp.maximum((b * nt + t) * hb - 1, 0), 0)
    nxt = lambda b, t: (jnp.minimum((b * nt + t + 1) * hb, n_hb - 1), 0)
    in_specs = [
        pl.BlockSpec((tile, V_W), row), pl.BlockSpec((tile, V_W), row), pl.BlockSpec((tile, V_W), row),
        pl.BlockSpec((tile, POOL_WIDTH), row), pl.BlockSpec((HALO, POOL_WIDTH), prev),
        pl.BlockSpec((HALO, POOL_WIDTH), nxt),
        pl.BlockSpec((tile, D_MODEL), row), pl.BlockSpec((tile, D_MODEL), row), pl.BlockSpec((tile, D_MODEL), row),
        _layer_spec((1, V_W), l), _layer_spec((POOL_GROUPS, POOL_GC, POOL_GC), l),
        _layer_spec((1, POOL_WIDTH), l),
        _layer_spec((V_W, D_MODEL), l), _layer_spec((POOL_WIDTH, D_MODEL), l),
        _layer_spec((D_MODEL, D_MODEL), l),
    ]
    block_bytes = (5 * _nbytes((tile, D_MODEL), BF16) + _nbytes((tile + 2 * HALO, POOL_WIDTH), BF16)
                   + 2 * _nbytes((tile, D_MODEL), F32))
    weight_bytes = (_nbytes((V_W + POOL_WIDTH + D_MODEL, D_MODEL), BF16)
                    + _nbytes((POOL_GROUPS, POOL_GC, POOL_GC), BF16))
    return pl.pallas_call(
        functools.partial(_mix_out_kernel, tile=tile, seq=seq),
        grid=(batch, nt),
        in_specs=in_specs,
        out_specs=pl.BlockSpec((tile, D_MODEL), row),
        out_shape=jax.ShapeDtypeStruct((n, D_MODEL), F32),
        compiler_params=pltpu.CompilerParams(
            dimension_semantics=("arbitrary", "arbitrary"),
            vmem_limit_bytes=_vmem_limit(block_bytes, weight_bytes)),
        name="mix_out",
    )(o_f, o_b, sr, u, u, u, sga, sgb, x, og, wgrp, ps, wa, wb, wo)


def _conv_ffn_kernel(x_ref, xp_ref, xn_ref, g2_ref, wup_ref, cw_ref, cb_ref, wdn_ref, gf_ref,
                     y_ref, xe_ref, hid_ref, *, tile, final_norm):
    t = pl.program_id(1)
    nt = pl.num_programs(1)
    g2 = g2_ref[...]
    x = x_ref[...]
    xe_ref[0:HALO, :] = jnp.where(t > 0, _rmsnorm(xp_ref[...], g2), 0.0).astype(BF16)
    xe_ref[HALO:HALO + tile, :] = _rmsnorm(x, g2).astype(BF16)
    xe_ref[HALO + tile:, :] = jnp.where(t < nt - 1, _rmsnorm(xn_ref[...], g2), 0.0).astype(BF16)
    ext = tile + 2 * HALO
    n_chunks = D_FF // FF_CHUNK

    def up(j):
        a = _dot(xe_ref[...], wup_ref[:, j * FF_CHUNK:(j + 1) * FF_CHUNK])
        val = _dot(xe_ref[HALO:HALO + tile, :], wup_ref[:, D_FF + j * FF_CHUNK:D_FF + (j + 1) * FF_CHUNK])
        return a, val

    def act(j, a, val):
        cs = slice(j * FF_CHUNK, (j + 1) * FF_CHUNK)
        cw = cw_ref[:, cs]
        conv = (pltpu.roll(a, 1, 0) * cw[0:1, :] + a * cw[1:2, :]
                + pltpu.roll(a, ext - 1, 0) * cw[2:3, :])[HALO:HALO + tile, :] + cb_ref[:, cs]
        hid_ref[:, cs] = (conv * jax.nn.sigmoid(conv) * val).astype(BF16)

    split = (n_chunks - 2) * FF_CHUNK
    nxt = up(0)
    down = None
    for j in range(n_chunks):
        cur = nxt
        if j + 1 < n_chunks:
            nxt = up(j + 1)
        if j == n_chunks - 2:
            down = _dot(hid_ref[:, :split], wdn_ref[:split, :])
        act(j, *cur)
    y = x + (down + _dot(hid_ref[:, split:], wdn_ref[split:, :]))
    if final_norm:
        y = _rmsnorm(y, gf_ref[...])
    y_ref[...] = y


def _conv_ffn(x, g2, wup, cw, cb, wdn, gf, l, batch, seq, tile, final_norm):
    n = batch * seq
    nt = seq // tile
    hb = tile // HALO
    n_hb = n // HALO
    row = lambda b, t: (b * nt + t, 0)
    prev = lambda b, t: (jnp.maximum((b * nt + t) * hb - 1, 0), 0)
    nxt = lambda b, t: (jnp.minimum((b * nt + t + 1) * hb, n_hb - 1), 0)
    in_specs = [
        pl.BlockSpec((tile, D_MODEL), row), pl.BlockSpec((HALO, D_MODEL), prev), pl.BlockSpec((HALO, D_MODEL), nxt),
        _layer_spec((1, D_MODEL), l),
        _layer_spec((D_MODEL, 2 * D_FF), l),
        _layer_spec((3, D_FF), l), _layer_spec((1, D_FF), l),
        _layer_spec((D_FF, D_MODEL), l),
        _layer_spec((1, D_MODEL), 0),
    ]
    scratch = [pltpu.VMEM((tile + 2 * HALO, D_MODEL), BF16), pltpu.VMEM((tile, D_FF), BF16)]
    block_bytes = 2 * _nbytes((tile + HALO, D_MODEL), F32)
    scratch_bytes = (_nbytes((D_MODEL, 2 * D_FF), BF16) + _nbytes((D_FF, D_MODEL), BF16)
                     + _nbytes((tile + 2 * HALO, D_MODEL), BF16) + _nbytes((tile, D_FF), BF16))
    return pl.pallas_call(
        functools.partial(_conv_ffn_kernel, tile=tile, final_norm=final_norm),
        grid=(batch, nt),
        in_specs=in_specs,
        out_specs=pl.BlockSpec((tile, D_MODEL), row),
        out_shape=jax.ShapeDtypeStruct((n, D_MODEL), F32),
        scratch_shapes=scratch,
        compiler_params=pltpu.CompilerParams(
            dimension_semantics=("arbitrary", "arbitrary"),
            vmem_limit_bytes=_vmem_limit(block_bytes, scratch_bytes)),
        name="conv_ffn",
    )(x, x, x, g2, wup, cw, cb, wdn, gf)


def _tiles(seq):
    small, big = min(512, seq), min(1024, seq)
    for tok in (small, big):
        assert seq % tok == 0 and tok % SUB_ROWS == 0 and tok % GLA_BLOCK == 0
    return small, big


def _prep_weights(norm1_g, w_in, w_lr2_f, b_lr_f, w_lr2_b, b_lr_b, onorm_g, w_pool_grp, pool_scale, w_br_a,
                  w_br_b, w_out, norm2_g, w_up, conv_w, conv_b, w_down, final_g):
    lr0 = 2 * QK_W + 2 * V_W
    lr1 = lr0 + 2 * GLA_LOWRANK
    depth = w_in.shape[0]
    wlr2 = jnp.zeros((depth, LR_PAD, 2 * QK_W), F32)
    wlr2 = wlr2.at[:, :GLA_LOWRANK, :QK_W].set(w_lr2_f)
    wlr2 = wlr2.at[:, GLA_LOWRANK:2 * GLA_LOWRANK, QK_W:].set(w_lr2_b)
    row = lambda a: a[:, None, :]
    return dict(
        wm=jnp.concatenate([w_in[:, :, :lr0], w_in[:, :, lr1:]], axis=2).astype(BF16),
        wlr=jnp.pad(w_in[:, :, lr0:lr1], ((0, 0), (0, 0), (0, LR_PAD - 2 * GLA_LOWRANK))).astype(BF16),
        wlr2=wlr2.astype(BF16), blr=row(jnp.concatenate([b_lr_f, b_lr_b], axis=1)),
        wgrp=w_pool_grp.astype(BF16), wa=w_br_a.astype(BF16), wb=w_br_b.astype(BF16), wo=w_out.astype(BF16),
        wup=w_up.astype(BF16), wdn=w_down.astype(BF16),
        g1=row(norm1_g), og=row(onorm_g), ps=row(pool_scale), g2=row(norm2_g), cw=conv_w, cb=row(conv_b),
        gf=final_g[None, None, :])


def _trunk(x, w):
    batch, seq, _ = x.shape
    tile_in, tile = _tiles(seq)
    depth = w["wm"].shape[0]
    x = x.reshape(batch * seq, D_MODEL)
    for l in range(depth):
        q, k, v, sr, u, sga, sgb, g, g_min = _in_proj(x, w["g1"], w["wm"], w["wlr"], w["wlr2"], w["blr"], l, tile_in)
        g_min = jnp.min(g_min[:, 0, 0].reshape(-1, tile // tile_in), axis=1)
        o_f, o_b = _gla(q, k, v, g, g_min, batch, seq, tile)
        x = _mix_out(o_f, o_b, sr, u, sga, sgb, x, w["og"], w["wgrp"], w["ps"], w["wa"], w["wb"], w["wo"],
                     l, batch, seq, tile)
        x = _conv_ffn(x, w["g2"], w["wup"], w["cw"], w["cb"], w["wdn"], w["gf"], l, batch, seq, tile,
                      final_norm=(l == depth - 1))
    return x.reshape(batch, seq, D_MODEL)


def kernel(x_prompt, x_sample, norm1_g, w_in, w_lr2_f, b_lr_f, w_lr2_b, b_lr_b, onorm_g, w_pool_grp, pool_scale, w_br_a, w_br_b, w_out, norm2_g, w_up, conv_w, conv_b, w_down, final_g):
    w = _prep_weights(norm1_g, w_in, w_lr2_f, b_lr_f, w_lr2_b, b_lr_b, onorm_g, w_pool_grp, pool_scale, w_br_a, w_br_b,
                      w_out, norm2_g, w_up, conv_w, conv_b, w_down, final_g)
    return (_trunk(x_prompt, w), _trunk(x_sample, w))
```

```python
import functools

import jax
import jax.numpy as jnp
from jax import lax
from jax.experimental import pallas as pl
from jax.experimental.pallas import tpu as pltpu

D_MODEL = 1024
GLA_HEADS = 4
GLA_DK = 128
GLA_DV = 256
QK_W = GLA_HEADS * GLA_DK
V_W = GLA_HEADS * GLA_DV
GLA_LOWRANK = 16
GLA_TAU = 16.0
GLA_BLOCK = 128
POOL_GROUPS = 4
POOL_WIDTH = 512
POOL_GC = 128
POOL_WINDOWS = (2, 4, 8, 16)
D_FF = 2816
EPS = 1e-6

V7X_VMEM_BYTES = 64 * 1024 * 1024
V7X_LANES = 128
V7X_MXU_DIM = 256
BF16_SUBLANE_TILE = 16
F32_SUBLANE_TILE = 8

ODD_PITCH_PAD = V7X_LANES
LR_PAD = V7X_LANES
HALO = BF16_SUBLANE_TILE
MAIN_W = 2 * QK_W + 2 * V_W + POOL_WIDTH + 2 * D_MODEL
FF_CHUNK = V7X_MXU_DIM
SUB_ROWS = 256
GLA_AHEAD = 4
EXP_RANGE = 80.0

F32 = jnp.float32
BF16 = jnp.bfloat16


def _vmem_limit(block_bytes, scratch_bytes):
    want = 2 * block_bytes + scratch_bytes + 16 * 1024 * 1024
    return int(min(want, V7X_VMEM_BYTES - 8 * 1024 * 1024))


def _nbytes(shape, dtype):
    n = 1
    for s in shape:
        n *= s
    return n * jnp.dtype(dtype).itemsize


def _rmsnorm(x, g):
    return x * lax.rsqrt(jnp.mean(x * x, axis=-1, keepdims=True) + EPS) * g


def _dot(a, b):
    return jnp.dot(a, b, preferred_element_type=F32)


def _dot_nt(a, b):
    return lax.dot_general(a, b, (((1,), (1,)), ((), ())), preferred_element_type=F32)


def _layer_spec(shape, l):
    zeros = (0,) * len(shape)
    return pl.BlockSpec((None,) + tuple(shape), lambda *_: (l,) + zeros, pipeline_mode=pl.Buffered(1))


def _in_proj_kernel(x_ref, g1_ref, wm_ref, wlr_ref, wlr2_ref, blr_ref,
                    q_ref, k_ref, v_ref, sr_ref, u_ref, sga_ref, sgb_ref, g_ref, gmin_ref, *, tile):
    sub = SUB_ROWS
    n_sub = tile // sub
    cw = 512
    g_min = None
    for i in range(n_sub):
        rows = slice(i * sub, (i + 1) * sub)
        xn = _rmsnorm(x_ref[rows, :], g1_ref[...]).astype(BF16)

        def proj(c0, w):
            return _dot(xn, wm_ref[:, c0:c0 + w])

        lr = _dot(xn, wlr_ref[...]).astype(BF16)
        q_ref[rows, :] = (proj(0, QK_W) * (GLA_DK ** -0.5)).astype(BF16)
        k_ref[rows, :] = proj(QK_W, QK_W).astype(BF16)
        c = 2 * QK_W
        for j in range(V_W // cw):
            v_ref[rows, j * cw:(j + 1) * cw] = proj(c + j * cw, cw).astype(BF16)
        c += V_W
        z = _dot(lr, wlr2_ref[:, :2 * QK_W]) + blr_ref[...]
        g = (jnp.minimum(z, 0.0) - jnp.log(1.0 + jnp.exp(-jnp.abs(z)))) * (1.0 / GLA_TAU)
        g_ref[rows, :] = g.astype(BF16)
        m = jnp.min(g)
        g_min = m if g_min is None else jnp.minimum(g_min, m)
        for j in range(V_W // cw):
            r = proj(c + j * cw, cw)
            sr_ref[rows, j * cw:(j + 1) * cw] = (r * jax.nn.sigmoid(r)).astype(BF16)
        c += V_W
        u_ref[rows, :] = proj(c, POOL_WIDTH).astype(BF16)
        c += POOL_WIDTH
        for j in range(D_MODEL // cw):
            sga_ref[rows, j * cw:(j + 1) * cw] = jax.nn.sigmoid(proj(c + j * cw, cw)).astype(BF16)
        c += D_MODEL
        for j in range(D_MODEL // cw):
            sgb_ref[rows, j * cw:(j + 1) * cw] = jax.nn.sigmoid(proj(c + j * cw, cw)).astype(BF16)
    gmin_ref[...] = jnp.full(gmin_ref.shape, g_min, F32)


def _in_proj(x, g1, wm, wlr, wlr2, blr, l, tile):
    n = x.shape[0]
    grid = (n // tile,)
    row = lambda i: (i, 0)
    outs = [(QK_W, BF16), (QK_W, BF16), (V_W, BF16), (V_W, BF16), (POOL_WIDTH, BF16),
            (D_MODEL, BF16), (D_MODEL, BF16), (2 * QK_W, BF16)]
    in_specs = [
        pl.BlockSpec((tile, D_MODEL), row),
        _layer_spec((1, D_MODEL), l),
        _layer_spec((D_MODEL, MAIN_W), l),
        _layer_spec((D_MODEL, LR_PAD), l),
        _layer_spec((LR_PAD, 2 * QK_W + ODD_PITCH_PAD), l),
        _layer_spec((1, 2 * QK_W), l),
    ]
    gmin_block = (1, F32_SUBLANE_TILE, V7X_LANES)
    block_bytes = _nbytes((tile, D_MODEL), F32) + sum(_nbytes((tile, w), d) for w, d in outs)
    weight_bytes = _nbytes((D_MODEL, MAIN_W + LR_PAD), BF16) + _nbytes((LR_PAD, 2 * QK_W), BF16)
    return pl.pallas_call(
        functools.partial(_in_proj_kernel, tile=tile),
        grid=grid,
        in_specs=in_specs,
        out_specs=[pl.BlockSpec((tile, w), row) for w, _ in outs]
        + [pl.BlockSpec(gmin_block, lambda i: (i, 0, 0))],
        out_shape=[jax.ShapeDtypeStruct((n, w), d) for w, d in outs]
        + [jax.ShapeDtypeStruct((n // tile,) + gmin_block[1:], F32)],
        compiler_params=pltpu.CompilerParams(
            dimension_semantics=("arbitrary",),
            vmem_limit_bytes=_vmem_limit(block_bytes, weight_bytes)),
        name="in_proj",
    )(x, g1, wm, wlr, wlr2, blr)


def _cum_log_decay(g, cum):
    return _dot(cum, g)


def _decay_columns(s_decay):
    return jnp.broadcast_to(s_decay, (GLA_BLOCK, QK_W)).T


def _scale_state(s, dec_t):
    return s * jnp.concatenate([dec_t] * (GLA_DV // GLA_BLOCK), axis=1)


def _gla_prep(r0, b, q_ref, k_ref, v_ref, edge_row):
    rows = slice(r0, r0 + GLA_BLOCK)
    s_decay = jnp.exp(b[edge_row:edge_row + 1, :])
    q_dec = (q_ref[rows, :].astype(F32) * jnp.exp(b)).astype(BF16)
    k_inv = k_ref[rows, :].astype(F32) * jnp.exp(-b)
    k_dec = (k_inv * s_decay).astype(BF16)
    return q_dec, k_inv.astype(BF16), k_dec.T, _decay_columns(s_decay), v_ref[rows, :]


def _gla_fast(fwd_refs, bwd_refs, of_ref, ob_ref, sf_ref, sb_ref, cums, masks, rows_f, rows_b):
    ks = [slice(h * GLA_DK, (h + 1) * GLA_DK) for h in range(GLA_HEADS)]
    vs = [slice(h * GLA_DV, (h + 1) * GLA_DV) for h in range(GLA_HEADS)]
    heads = [(d, h) for d in range(2) for h in range(GLA_HEADS)]
    o_refs = (of_ref, ob_ref)
    state = {(d, h): (sf_ref, sb_ref)[d][h] for d, h in heads}

    def outputs(blocks, scores, pre_state):
        for d, h in heads:
            r0, (q_dec, _, _, _, v) = blocks[d]
            a = jnp.where(masks[d], scores[d, h], 0.0).astype(BF16)
            lhs = jnp.concatenate([q_dec[:, ks[h]], a], axis=1)
            rhs = jnp.concatenate([pre_state[d, h].astype(BF16), v[:, vs[h]]], axis=0)
            o_refs[d][r0:r0 + GLA_BLOCK, vs[h]] = _dot(lhs, rhs).astype(o_refs[d].dtype)

    n = len(rows_f)
    b_f, b_b = {}, {}

    def cum_decays(steps):
        for s in steps:
            b_f[s] = _cum_log_decay(fwd_refs[3][rows_f[s]:rows_f[s] + GLA_BLOCK, :], cums[0])
            b_b[s] = _cum_log_decay(bwd_refs[3][rows_b[s]:rows_b[s] + GLA_BLOCK, :], cums[1])

    cum_decays(range(min(GLA_AHEAD, n)))
    pending = None
    for step, (rf, rb) in enumerate(zip(rows_f, rows_b)):
        if step % GLA_AHEAD == GLA_AHEAD - 2 and step + 2 < n:
            cum_decays(range(step + 2, min(step + 2 + GLA_AHEAD, n)))
        blocks = ((rf, _gla_prep(rf, b_f[step], *fwd_refs[:3], GLA_BLOCK - 1)),
                  (rb, _gla_prep(rb, b_b[step], *bwd_refs[:3], 0)))
        scores, upd = {}, {}
        for d, h in heads:
            q_dec, k_inv = blocks[d][1][0], blocks[d][1][1]
            scores[d, h] = _dot_nt(q_dec[:, ks[h]], k_inv[:, ks[h]])
        for d, h in heads:
            kd_t, v = blocks[d][1][2], blocks[d][1][4]
            upd[d, h] = _dot(kd_t[ks[h], :], v[:, vs[h]])
        if pending is not None:
            outputs(*pending)
        pending = (blocks, scores, dict(state))
        for d, h in heads:
            dec_t = blocks[d][1][3]
            state[d, h] = _scale_state(state[d, h], dec_t[ks[h], :]) + upd[d, h]
    outputs(*pending)
    for d, h in heads:
        (sf_ref, sb_ref)[d][h] = state[d, h]


def _gla_exact(refs, o_ref, s_ref, cum, mask, edge_row, reverse, n_blocks, b_scr, q_scr, k_scr, sc_scr):
    q_ref, k_ref, v_ref, g_ref = refs
    ks = [slice(h * GLA_DK, (h + 1) * GLA_DK) for h in range(GLA_HEADS)]
    vs = [slice(h * GLA_DV, (h + 1) * GLA_DV) for h in range(GLA_HEADS)]
    lane = lax.broadcasted_iota(jnp.int32, (GLA_BLOCK, GLA_BLOCK), 1)

    def block(c, carry):
        blk = (n_blocks - 1 - c) if reverse else c
        rows = pl.ds(pl.multiple_of(blk * GLA_BLOCK, GLA_BLOCK), GLA_BLOCK)
        b = _cum_log_decay(g_ref[rows, :], cum)
        q = q_ref[rows, :].astype(F32)
        k = k_ref[rows, :].astype(F32)
        b_scr[...] = b
        q_scr[...] = q
        k_scr[...] = k
        sc_scr[...] = jnp.zeros_like(sc_scr)

        def column(j, carry):
            w = jnp.exp(jnp.minimum(b_scr[...] - b_scr[pl.ds(j, 1), :], 0.0))
            contrib = q_scr[...] * w * k_scr[pl.ds(j, 1), :]
            for h in range(GLA_HEADS):
                col = jnp.sum(contrib[:, ks[h]], axis=1, keepdims=True)
                sc_scr[h] += jnp.where(lane == j, col, 0.0)
            return carry

        lax.fori_loop(0, GLA_BLOCK, column, 0)
        b_edge = b[edge_row:edge_row + 1, :]
        s_decay = jnp.exp(b_edge)
        q_dec = (q * jnp.exp(b)).astype(BF16)
        kd_t = (k * jnp.exp(b_edge - b)).astype(BF16).T
        dec_t = _decay_columns(s_decay)
        v = v_ref[rows, :]
        for h in range(GLA_HEADS):
            s = s_ref[h]
            a = jnp.where(mask, sc_scr[h], 0.0).astype(BF16)
            lhs = jnp.concatenate([q_dec[:, ks[h]], a], axis=1)
            rhs = jnp.concatenate([s.astype(BF16), v[:, vs[h]]], axis=0)
            o_ref[rows, vs[h]] = _dot(lhs, rhs).astype(o_ref.dtype)
            s_ref[h] = _scale_state(s, dec_t[ks[h], :]) + _dot(kd_t[ks[h], :], v[:, vs[h]])
        return carry

    lax.fori_loop(0, n_blocks, block, 0)


def _gla_kernel(gmin_ref, qf_ref, kf_ref, vf_ref, gf_ref, qb_ref, kb_ref, vb_ref, gb_ref,
                of_ref, ob_ref, sf_ref, sb_ref, b_scr, q_scr, k_scr, sc_scr, *, n_blocks):
    bi = pl.program_id(0)
    t = pl.program_id(1)
    nt = pl.num_programs(1)

    @pl.when(t == 0)
    def _():
        sf_ref[...] = jnp.zeros_like(sf_ref)
        sb_ref[...] = jnp.zeros_like(sb_ref)

    ri = lax.broadcasted_iota(jnp.int32, (GLA_BLOCK, GLA_BLOCK), 0)
    ci = lax.broadcasted_iota(jnp.int32, (GLA_BLOCK, GLA_BLOCK), 1)
    masks = (ri >= ci, ri <= ci)
    cums = (masks[0].astype(BF16), masks[1].astype(BF16))
    fwd_refs = (qf_ref, kf_ref, vf_ref, gf_ref)
    bwd_refs = (qb_ref, kb_ref, vb_ref, gb_ref)

    g_min = jnp.minimum(gmin_ref[bi * nt + t], gmin_ref[bi * nt + (nt - 1 - t)])
    factorisable = g_min * GLA_BLOCK > -EXP_RANGE

    @pl.when(factorisable)
    def _():
        offs = [c * GLA_BLOCK for c in range(n_blocks)]
        _gla_fast(fwd_refs, bwd_refs, of_ref, ob_ref, sf_ref, sb_ref, cums, masks, offs, offs[::-1])

    @pl.when(jnp.logical_not(factorisable))
    def _():
        scr = (b_scr, q_scr, k_scr, sc_scr)
        _gla_exact(fwd_refs, of_ref, sf_ref, cums[0], masks[0], GLA_BLOCK - 1, False, n_blocks, *scr)
        _gla_exact(bwd_refs, ob_ref, sb_ref, cums[1], masks[1], 0, True, n_blocks, *scr)


def _gla(q, k, v, g, g_min, batch, seq, tile):
    n = batch * seq
    nt = seq // tile
    fwd = lambda b, t, gm: (b * nt + t, 0)
    bwd = lambda b, t, gm: (b * nt + (nt - 1 - t), 0)
    bwd_g = lambda b, t, gm: (b * nt + (nt - 1 - t), 1)
    in_specs = [
        pl.BlockSpec((tile, QK_W), fwd), pl.BlockSpec((tile, QK_W), fwd),
        pl.BlockSpec((tile, V_W), fwd), pl.BlockSpec((tile, QK_W), fwd),
        pl.BlockSpec((tile, QK_W), bwd), pl.BlockSpec((tile, QK_W), bwd),
        pl.BlockSpec((tile, V_W), bwd), pl.BlockSpec((tile, QK_W), bwd_g),
    ]
    state = pltpu.VMEM((GLA_HEADS, GLA_DK, GLA_DV), F32)
    rows_f32 = pltpu.VMEM((GLA_BLOCK, QK_W), F32)
    scratch = [state, state, rows_f32, rows_f32, rows_f32, pltpu.VMEM((GLA_HEADS, GLA_BLOCK, GLA_BLOCK), F32)]
    block_bytes = 2 * (2 * _nbytes((tile, QK_W), BF16) + 2 * _nbytes((tile, V_W), BF16)
                       + _nbytes((tile, QK_W), BF16))
    scratch_bytes = (2 * _nbytes((GLA_HEADS, GLA_DV, GLA_DK), F32) + 3 * _nbytes((GLA_BLOCK, QK_W), F32)
                     + _nbytes((GLA_HEADS, GLA_BLOCK, GLA_BLOCK), F32))
    grid_spec = pltpu.PrefetchScalarGridSpec(
        num_scalar_prefetch=1,
        grid=(batch, nt),
        in_specs=in_specs,
        out_specs=[pl.BlockSpec((tile, V_W), fwd), pl.BlockSpec((tile, V_W), bwd)],
        scratch_shapes=scratch,
    )
    return pl.pallas_call(
        functools.partial(_gla_kernel, n_blocks=tile // GLA_BLOCK),
        grid_spec=grid_spec,
        out_shape=[jax.ShapeDtypeStruct((n, V_W), BF16)] * 2,
        compiler_params=pltpu.CompilerParams(
            dimension_semantics=("arbitrary", "arbitrary"),
            vmem_limit_bytes=_vmem_limit(block_bytes, scratch_bytes)),
        name="gla",
    )(g_min, q, k, v, g, q, k, v, g)


def _mix_out_kernel(of_ref, ob_ref, sr_ref, u_ref, up_ref, un_ref, sga_ref, sgb_ref, x_ref,
                    og_ref, wgrp_ref, ps_ref, wao_ref, wb_ref, y_ref, *, tile, seq):
    t = pl.program_id(1)
    nt = pl.num_programs(1)
    sub = SUB_ROWS
    n_sub = tile // sub

    u_prev = jnp.where(t > 0, up_ref[...].astype(F32), 0.0)
    u_next = jnp.where(t < nt - 1, un_ref[...].astype(F32), 0.0)
    u_ext = jnp.concatenate([u_prev, u_ref[...].astype(F32), u_next], axis=0)

    def window_mean(centred, w, ahead, r0):
        def edge(e0):
            pos = t * tile + r0 + e0 + lax.broadcasted_iota(jnp.int32, (HALO, POOL_GC), 0)
            cnt = jnp.minimum(pos + ahead, seq - 1) - jnp.maximum(pos - w // 2, 0) + 1
            return centred[e0:e0 + HALO, :] / cnt.astype(F32)
        inner = centred[HALO:sub - HALO, :] * (1.0 / w)
        return jnp.concatenate([edge(0), inner, edge(sub - HALO)], axis=0)

    def branches(i):
        r0 = i * sub
        rows = slice(r0, r0 + sub)
        o = of_ref[rows, :].astype(F32) + ob_ref[rows, :].astype(F32)
        normed = []
        for h in range(GLA_HEADS):
            oh = o[:, h * GLA_DV:(h + 1) * GLA_DV]
            normed.append(oh * lax.rsqrt(jnp.mean(oh * oh, axis=-1, keepdims=True) + EPS))
        a_in = (jnp.concatenate(normed, axis=-1) * og_ref[...]).astype(BF16) * sr_ref[rows, :]
        a_out = _dot(a_in, wao_ref[:, :D_MODEL])
        ext = sub + 2 * HALO
        pooled = []
        for gi, w in enumerate(POOL_WINDOWS):
            ug = u_ext[r0:r0 + ext, gi * POOL_GC:(gi + 1) * POOL_GC]
            trail = ug
            span = 1
            while span < w:
                trail = trail + pltpu.roll(trail, span, 0)
                span *= 2
            ahead = w - 1 - w // 2
            centred = pltpu.roll(trail, ext - ahead, 0) if ahead else trail
            p = window_mean(centred[HALO:HALO + sub, :], w, ahead, r0) - ug[HALO:HALO + sub, :]
            pooled.append(_dot(p.astype(BF16), wgrp_ref[gi]))
        pb = (jnp.concatenate(pooled, axis=-1) * ps_ref[...]).astype(BF16)
        b_out = _dot(pb, wb_ref[:, :D_MODEL])
        return a_out, b_out

    def merge(i, a_out, b_out):
        rows = slice(i * sub, (i + 1) * sub)
        merged = sga_ref[rows, :] * a_out.astype(BF16) + sgb_ref[rows, :] * b_out.astype(BF16)
        y_ref[rows, :] = x_ref[rows, :] + _dot(merged, wao_ref[:, D_MODEL:2 * D_MODEL])

    pending = branches(0)
    for i in range(1, n_sub):
        nxt = branches(i)
        merge(i - 1, *pending)
        pending = nxt
    merge(n_sub - 1, *pending)


def _mix_out(o_f, o_b, sr, u, sga, sgb, x, og, wgrp, ps, wao, wb, l, batch, seq, tile):
    n = batch * seq
    nt = seq // tile
    hb = tile // HALO
    n_hb = n // HALO
    row = lambda b, t: (b * nt + t, 0)
    prev = lambda b, t: (jnp.maximum((b * nt + t) * hb - 1, 0), 0)
    nxt = lambda b, t: (jnp.minimum((b * nt + t + 1) * hb, n_hb - 1), 0)
    in_specs = [
        pl.BlockSpec((tile, V_W), row), pl.BlockSpec((tile, V_W), row), pl.BlockSpec((tile, V_W), row),
        pl.BlockSpec((tile, POOL_WIDTH), row), pl.BlockSpec((HALO, POOL_WIDTH), prev),
        pl.BlockSpec((HALO, POOL_WIDTH), nxt),
        pl.BlockSpec((tile, D_MODEL), row), pl.BlockSpec((tile, D_MODEL), row), pl.BlockSpec((tile, D_MODEL), row),
        _layer_spec((1, V_W), l), _layer_spec((POOL_GROUPS, POOL_GC, POOL_GC), l),
        _layer_spec((1, POOL_WIDTH), l),
        _layer_spec((V_W, 2 * D_MODEL + 2 * ODD_PITCH_PAD), l), _layer_spec((POOL_WIDTH, D_MODEL + ODD_PITCH_PAD), l),
    ]
    block_bytes = (5 * _nbytes((tile, D_MODEL), BF16) + _nbytes((tile + 2 * HALO, POOL_WIDTH), BF16)
                   + 2 * _nbytes((tile, D_MODEL), F32))
    weight_bytes = (_nbytes((V_W + POOL_WIDTH + D_MODEL, D_MODEL), BF16)
                    + _nbytes((POOL_GROUPS, POOL_GC, POOL_GC), BF16))
    return pl.pallas_call(
        functools.partial(_mix_out_kernel, tile=tile, seq=seq),
        grid=(batch, nt),
        in_specs=in_specs,
        out_specs=pl.BlockSpec((tile, D_MODEL), row),
        out_shape=jax.ShapeDtypeStruct((n, D_MODEL), F32),
        compiler_params=pltpu.CompilerParams(
            dimension_semantics=("arbitrary", "arbitrary"),
            vmem_limit_bytes=_vmem_limit(block_bytes, weight_bytes)),
        name="mix_out",
    )(o_f, o_b, sr, u, u, u, sga, sgb, x, og, wgrp, ps, wao, wb)


def _conv_ffn_kernel(x_ref, xp_ref, xn_ref, g2_ref, wup_ref, cw_ref, cb_ref, wdn_ref, gf_ref,
                     y_ref, xe_ref, hid_ref, *, tile, final_norm):
    t = pl.program_id(1)
    nt = pl.num_programs(1)
    g2 = g2_ref[...]
    x = x_ref[...]
    xe_ref[0:HALO, :] = jnp.where(t > 0, _rmsnorm(xp_ref[...], g2), 0.0).astype(BF16)
    xe_ref[HALO:HALO + tile, :] = _rmsnorm(x, g2).astype(BF16)
    xe_ref[HALO + tile:, :] = jnp.where(t < nt - 1, _rmsnorm(xn_ref[...], g2), 0.0).astype(BF16)
    ext = tile + 2 * HALO
    n_chunks = D_FF // FF_CHUNK

    def up(j):
        a = _dot(xe_ref[...], wup_ref[:, j * FF_CHUNK:(j + 1) * FF_CHUNK])
        val = _dot(xe_ref[HALO:HALO + tile, :], wup_ref[:, D_FF + j * FF_CHUNK:D_FF + (j + 1) * FF_CHUNK])
        return a, val

    def act(j, a, val):
        cs = slice(j * FF_CHUNK, (j + 1) * FF_CHUNK)
        cw = cw_ref[:, cs]
        conv = (pltpu.roll(a, 1, 0) * cw[0:1, :] + a * cw[1:2, :]
                + pltpu.roll(a, ext - 1, 0) * cw[2:3, :])[HALO:HALO + tile, :] + cb_ref[:, cs]
        hid_ref[:, cs] = (conv * jax.nn.sigmoid(conv) * val).astype(BF16)

    split = (n_chunks - 2) * FF_CHUNK
    nxt = up(0)
    down = None
    for j in range(n_chunks):
        cur = nxt
        if j + 1 < n_chunks:
            nxt = up(j + 1)
        if j == n_chunks - 2:
            down = _dot(hid_ref[:, :split], wdn_ref[:split, :D_MODEL])
        act(j, *cur)
    y = x + (down + _dot(hid_ref[:, split:], wdn_ref[split:, :D_MODEL]))
    if final_norm:
        y = _rmsnorm(y, gf_ref[...])
    y_ref[...] = y


def _conv_ffn(x, g2, wup, cw, cb, wdn, gf, l, batch, seq, tile, final_norm):
    n = batch * seq
    nt = seq // tile
    hb = tile // HALO
    n_hb = n // HALO
    row = lambda b, t: (b * nt + t, 0)
    prev = lambda b, t: (jnp.maximum((b * nt + t) * hb - 1, 0), 0)
    nxt = lambda b, t: (jnp.minimum((b * nt + t + 1) * hb, n_hb - 1), 0)
    in_specs = [
        pl.BlockSpec((tile, D_MODEL), row), pl.BlockSpec((HALO, D_MODEL), prev), pl.BlockSpec((HALO, D_MODEL), nxt),
        _layer_spec((1, D_MODEL), l),
        _layer_spec((D_MODEL, 2 * D_FF), l),
        _layer_spec((3, D_FF), l), _layer_spec((1, D_FF), l),
        _layer_spec((D_FF, D_MODEL + ODD_PITCH_PAD), l),
        _layer_spec((1, D_MODEL), 0),
    ]
    scratch = [pltpu.VMEM((tile + 2 * HALO, D_MODEL), BF16), pltpu.VMEM((tile, D_FF), BF16)]
    block_bytes = 2 * _nbytes((tile + HALO, D_MODEL), F32)
    scratch_bytes = (_nbytes((D_MODEL, 2 * D_FF), BF16) + _nbytes((D_FF, D_MODEL), BF16)
                     + _nbytes((tile + 2 * HALO, D_MODEL), BF16) + _nbytes((tile, D_FF), BF16))
    return pl.pallas_call(
        functools.partial(_conv_ffn_kernel, tile=tile, final_norm=final_norm),
        grid=(batch, nt),
        in_specs=in_specs,
        out_specs=pl.BlockSpec((tile, D_MODEL), row),
        out_shape=jax.ShapeDtypeStruct((n, D_MODEL), F32),
        scratch_shapes=scratch,
        compiler_params=pltpu.CompilerParams(
            dimension_semantics=("arbitrary", "arbitrary"),
            vmem_limit_bytes=_vmem_limit(block_bytes, scratch_bytes)),
        name="conv_ffn",
    )(x, x, x, g2, wup, cw, cb, wdn, gf)


def _tiles(seq):
    small, big = min(1024, seq), min(1024, seq)
    for tok in (small, big):
        assert seq % tok == 0 and tok % SUB_ROWS == 0 and tok % GLA_BLOCK == 0
    return small, big


def _prep_weights(norm1_g, w_in, w_lr2_f, b_lr_f, w_lr2_b, b_lr_b, onorm_g, w_pool_grp, pool_scale, w_br_a,
                  w_br_b, w_out, norm2_g, w_up, conv_w, conv_b, w_down, final_g):
    lr0 = 2 * QK_W + 2 * V_W
    lr1 = lr0 + 2 * GLA_LOWRANK
    depth = w_in.shape[0]
    wlr2 = jnp.zeros((depth, LR_PAD, 2 * QK_W + ODD_PITCH_PAD), F32)
    wlr2 = wlr2.at[:, :GLA_LOWRANK, :QK_W].set(w_lr2_f)
    wlr2 = wlr2.at[:, GLA_LOWRANK:2 * GLA_LOWRANK, QK_W:2 * QK_W].set(w_lr2_b)
    row = lambda a: a[:, None, :]
    wide = lambda a, n: jnp.pad(a, ((0, 0), (0, 0), (0, n * ODD_PITCH_PAD))).astype(BF16)
    return dict(
        wm=jnp.concatenate([w_in[:, :, :lr0], w_in[:, :, lr1:]], axis=2).astype(BF16),
        wlr=jnp.pad(w_in[:, :, lr0:lr1], ((0, 0), (0, 0), (0, LR_PAD - 2 * GLA_LOWRANK))).astype(BF16),
        wlr2=wlr2.astype(BF16), blr=row(jnp.concatenate([b_lr_f, b_lr_b], axis=1)),
        wgrp=w_pool_grp.astype(BF16), wao=wide(jnp.concatenate([w_br_a, w_out], axis=2), 2), wb=wide(w_br_b, 1),
        wup=w_up.astype(BF16), wdn=wide(w_down, 1),
        g1=row(norm1_g), og=row(onorm_g), ps=row(pool_scale), g2=row(norm2_g), cw=conv_w, cb=row(conv_b),
        gf=final_g[None, None, :])


def _trunk(x, w):
    batch, seq, _ = x.shape
    tile_in, tile = _tiles(seq)
    depth = w["wm"].shape[0]
    x = x.reshape(batch * seq, D_MODEL)
    for l in range(depth):
        q, k, v, sr, u, sga, sgb, g, g_min = _in_proj(x, w["g1"], w["wm"], w["wlr"], w["wlr2"], w["blr"], l, tile_in)
        g_min = jnp.min(g_min[:, 0, 0].reshape(-1, tile // tile_in), axis=1)
        o_f, o_b = _gla(q, k, v, g, g_min, batch, seq, tile)
        x = _mix_out(o_f, o_b, sr, u, sga, sgb, x, w["og"], w["wgrp"], w["ps"], w["wao"], w["wb"],
                     l, batch, seq, tile)
        x = _conv_ffn(x, w["g2"], w["wup"], w["cw"], w["cb"], w["wdn"], w["gf"], l, batch, seq, tile,
                      final_norm=(l == depth - 1))
    return x.reshape(batch, seq, D_MODEL)


def kernel(x_prompt, x_sample, norm1_g, w_in, w_lr2_f, b_lr_f, w_lr2_b, b_lr_b, onorm_g, w_pool_grp, pool_scale, w_br_a, w_br_b, w_out, norm2_g, w_up, conv_w, conv_b, w_down, final_g):
    w = _prep_weights(norm1_g, w_in, w_lr2_f, b_lr_f, w_lr2_b, b_lr_b, onorm_g, w_pool_grp, pool_scale, w_br_a, w_br_b,
                      w_out, norm2_g, w_up, conv_w, conv_b, w_down, final_g)
    return (_trunk(x_prompt, w), _trunk(x_sample, w))
```

```python
import functools

import jax
import jax.numpy as jnp
from jax import lax
from jax.experimental import pallas as pl
from jax.experimental.pallas import tpu as pltpu

D_MODEL = 1024
GLA_HEADS = 4
GLA_DK = 128
GLA_DV = 256
QK_W = GLA_HEADS * GLA_DK
V_W = GLA_HEADS * GLA_DV
GLA_LOWRANK = 16
GLA_TAU = 16.0
GLA_BLOCK = 128
POOL_GROUPS = 4
POOL_WIDTH = 512
POOL_GC = 128
POOL_WINDOWS = (2, 4, 8, 16)
D_FF = 2816
EPS = 1e-6

V7X_VMEM_BYTES = 64 * 1024 * 1024
V7X_LANES = 128
V7X_MXU_DIM = 256
BF16_SUBLANE_TILE = 16
F32_SUBLANE_TILE = 8

ODD_PITCH_PAD = V7X_LANES
LR_PAD = V7X_LANES
HALO = BF16_SUBLANE_TILE
MAIN_W = 2 * QK_W + 2 * V_W + POOL_WIDTH + 2 * D_MODEL
FF_CHUNK = V7X_MXU_DIM
SUB_ROWS = 256
GLA_AHEAD = 4
EXP_RANGE = 80.0

F32 = jnp.float32
BF16 = jnp.bfloat16


def _vmem_limit(block_bytes, scratch_bytes):
    want = 2 * block_bytes + scratch_bytes + 16 * 1024 * 1024
    return int(min(want, V7X_VMEM_BYTES - 8 * 1024 * 1024))


def _nbytes(shape, dtype):
    n = 1
    for s in shape:
        n *= s
    return n * jnp.dtype(dtype).itemsize


def _rmsnorm(x, g):
    return x * lax.rsqrt(jnp.mean(x * x, axis=-1, keepdims=True) + EPS) * g


def _dot(a, b):
    return jnp.dot(a, b, preferred_element_type=F32)


def _dot_nt(a, b):
    return lax.dot_general(a, b, (((1,), (1,)), ((), ())), preferred_element_type=F32)


def _layer_spec(shape, l):
    zeros = (0,) * len(shape)
    return pl.BlockSpec((None,) + tuple(shape), lambda *_: (l,) + zeros, pipeline_mode=pl.Buffered(1))


def _in_proj_kernel(x_ref, g1_ref, wm_ref, wlr_ref, wlr2_ref, blr_ref,
                    q_ref, k_ref, v_ref, sr_ref, u_ref, sga_ref, sgb_ref, g_ref, gmin_ref, *, tile):
    sub = SUB_ROWS
    n_sub = tile // sub
    cw = 512
    g_min = None
    for i in range(n_sub):
        rows = slice(i * sub, (i + 1) * sub)
        xn = _rmsnorm(x_ref[rows, :], g1_ref[...]).astype(BF16)

        def proj(c0, w):
            return _dot(xn, wm_ref[:, c0:c0 + w])

        lr = _dot(xn, wlr_ref[...]).astype(BF16)
        q_ref[rows, :] = (proj(0, QK_W) * (GLA_DK ** -0.5)).astype(BF16)
        k_ref[rows, :] = proj(QK_W, QK_W).astype(BF16)
        c = 2 * QK_W
        for j in range(V_W // cw):
            v_ref[rows, j * cw:(j + 1) * cw] = proj(c + j * cw, cw).astype(BF16)
        c += V_W
        z = _dot(lr, wlr2_ref[:, :2 * QK_W]) + blr_ref[...]
        g = (jnp.minimum(z, 0.0) - jnp.log(1.0 + jnp.exp(-jnp.abs(z)))) * (1.0 / GLA_TAU)
        g_ref[rows, :] = g.astype(BF16)
        m = jnp.min(g)
        g_min = m if g_min is None else jnp.minimum(g_min, m)
        for j in range(V_W // cw):
            r = proj(c + j * cw, cw)
            sr_ref[rows, j * cw:(j + 1) * cw] = (r * jax.nn.sigmoid(r)).astype(BF16)
        c += V_W
        u_ref[rows, :] = proj(c, POOL_WIDTH).astype(BF16)
        c += POOL_WIDTH
        for j in range(D_MODEL // cw):
            sga_ref[rows, j * cw:(j + 1) * cw] = jax.nn.sigmoid(proj(c + j * cw, cw)).astype(BF16)
        c += D_MODEL
        for j in range(D_MODEL // cw):
            sgb_ref[rows, j * cw:(j + 1) * cw] = jax.nn.sigmoid(proj(c + j * cw, cw)).astype(BF16)
    gmin_ref[...] = jnp.full(gmin_ref.shape, g_min, F32)


def _in_proj(x, g1, wm, wlr, wlr2, blr, l, tile):
    n = x.shape[0]
    grid = (n // tile,)
    row = lambda i: (i, 0)
    outs = [(QK_W, BF16), (QK_W, BF16), (V_W, BF16), (V_W, BF16), (POOL_WIDTH, BF16),
            (D_MODEL, BF16), (D_MODEL, BF16), (2 * QK_W, BF16)]
    in_specs = [
        pl.BlockSpec((tile, D_MODEL), row),
        _layer_spec((1, D_MODEL), l),
        _layer_spec((D_MODEL, MAIN_W), l),
        _layer_spec((D_MODEL, LR_PAD), l),
        _layer_spec((LR_PAD, 2 * QK_W + ODD_PITCH_PAD), l),
        _layer_spec((1, 2 * QK_W), l),
    ]
    gmin_block = (1, F32_SUBLANE_TILE, V7X_LANES)
    block_bytes = _nbytes((tile, D_MODEL), F32) + sum(_nbytes((tile, w), d) for w, d in outs)
    weight_bytes = _nbytes((D_MODEL, MAIN_W + LR_PAD), BF16) + _nbytes((LR_PAD, 2 * QK_W), BF16)
    return pl.pallas_call(
        functools.partial(_in_proj_kernel, tile=tile),
        grid=grid,
        in_specs=in_specs,
        out_specs=[pl.BlockSpec((tile, w), row) for w, _ in outs]
        + [pl.BlockSpec(gmin_block, lambda i: (i, 0, 0))],
        out_shape=[jax.ShapeDtypeStruct((n, w), d) for w, d in outs]
        + [jax.ShapeDtypeStruct((n // tile,) + gmin_block[1:], F32)],
        compiler_params=pltpu.CompilerParams(
            dimension_semantics=("arbitrary",),
            vmem_limit_bytes=_vmem_limit(block_bytes, weight_bytes)),
        name="in_proj",
    )(x, g1, wm, wlr, wlr2, blr)


def _cum_log_decay(g, cum):
    return _dot(cum, g)


def _decay_columns(s_decay):
    return jnp.broadcast_to(s_decay, (GLA_BLOCK, QK_W)).T


def _scale_state(s, dec_t):
    return s * jnp.concatenate([dec_t] * (GLA_DV // GLA_BLOCK), axis=1)


def _gla_prep(r0, b, q_ref, k_ref, v_ref, edge_row):
    rows = slice(r0, r0 + GLA_BLOCK)
    s_decay = jnp.exp(b[edge_row:edge_row + 1, :])
    e = jnp.exp(b)
    q_dec = q_ref[rows, :] * e.astype(BF16)
    k_inv = k_ref[rows, :] * (1.0 / e).astype(BF16)
    k_dec = k_inv * s_decay.astype(BF16)
    return q_dec, k_inv, k_dec.T, _decay_columns(s_decay), v_ref[rows, :]


def _gla_fast(fwd_refs, bwd_refs, of_ref, ob_ref, sf_ref, sb_ref, cums, masks, rows_f, rows_b):
    ks = [slice(h * GLA_DK, (h + 1) * GLA_DK) for h in range(GLA_HEADS)]
    vs = [slice(h * GLA_DV, (h + 1) * GLA_DV) for h in range(GLA_HEADS)]
    heads = [(d, h) for d in range(2) for h in range(GLA_HEADS)]
    o_refs = (of_ref, ob_ref)
    state = {(d, h): (sf_ref, sb_ref)[d][h] for d, h in heads}

    def outputs(blocks, scores, pre_state):
        for d, h in heads:
            r0, (q_dec, _, _, _, v) = blocks[d]
            a = jnp.where(masks[d], scores[d, h], 0.0).astype(BF16)
            lhs = jnp.concatenate([q_dec[:, ks[h]], a], axis=1)
            rhs = jnp.concatenate([pre_state[d, h].astype(BF16), v[:, vs[h]]], axis=0)
            o_refs[d][r0:r0 + GLA_BLOCK, vs[h]] = _dot(lhs, rhs).astype(o_refs[d].dtype)

    n = len(rows_f)
    b_f, b_b = {}, {}

    def cum_decays(steps):
        for s in steps:
            b_f[s] = _cum_log_decay(fwd_refs[3][rows_f[s]:rows_f[s] + GLA_BLOCK, :], cums[0])
            b_b[s] = _cum_log_decay(bwd_refs[3][rows_b[s]:rows_b[s] + GLA_BLOCK, :], cums[1])

    cum_decays(range(min(GLA_AHEAD, n)))
    pending = None
    for step, (rf, rb) in enumerate(zip(rows_f, rows_b)):
        if step % GLA_AHEAD == GLA_AHEAD - 2 and step + 2 < n:
            cum_decays(range(step + 2, min(step + 2 + GLA_AHEAD, n)))
        blocks = ((rf, _gla_prep(rf, b_f[step], *fwd_refs[:3], GLA_BLOCK - 1)),
                  (rb, _gla_prep(rb, b_b[step], *bwd_refs[:3], 0)))
        scores, upd = {}, {}
        for d, h in heads:
            q_dec, k_inv = blocks[d][1][0], blocks[d][1][1]
            scores[d, h] = _dot_nt(q_dec[:, ks[h]], k_inv[:, ks[h]])
        for d, h in heads:
            kd_t, v = blocks[d][1][2], blocks[d][1][4]
            upd[d, h] = _dot(kd_t[ks[h], :], v[:, vs[h]])
        if pending is not None:
            outputs(*pending)
        pending = (blocks, scores, dict(state))
        for d, h in heads:
            dec_t = blocks[d][1][3]
            state[d, h] = _scale_state(state[d, h], dec_t[ks[h], :]) + upd[d, h]
    outputs(*pending)
    for d, h in heads:
        (sf_ref, sb_ref)[d][h] = state[d, h]


def _gla_exact(refs, o_ref, s_ref, cum, mask, edge_row, reverse, n_blocks, b_scr, q_scr, k_scr, sc_scr):
    q_ref, k_ref, v_ref, g_ref = refs
    ks = [slice(h * GLA_DK, (h + 1) * GLA_DK) for h in range(GLA_HEADS)]
    vs = [slice(h * GLA_DV, (h + 1) * GLA_DV) for h in range(GLA_HEADS)]
    lane = lax.broadcasted_iota(jnp.int32, (GLA_BLOCK, GLA_BLOCK), 1)

    def block(c, carry):
        blk = (n_blocks - 1 - c) if reverse else c
        rows = pl.ds(pl.multiple_of(blk * GLA_BLOCK, GLA_BLOCK), GLA_BLOCK)
        b = _cum_log_decay(g_ref[rows, :], cum)
        q = q_ref[rows, :].astype(F32)
        k = k_ref[rows, :].astype(F32)
        b_scr[...] = b
        q_scr[...] = q
        k_scr[...] = k
        sc_scr[...] = jnp.zeros_like(sc_scr)

        def column(j, carry):
            w = jnp.exp(jnp.minimum(b_scr[...] - b_scr[pl.ds(j, 1), :], 0.0))
            contrib = q_scr[...] * w * k_scr[pl.ds(j, 1), :]
            for h in range(GLA_HEADS):
                col = jnp.sum(contrib[:, ks[h]], axis=1, keepdims=True)
                sc_scr[h] += jnp.where(lane == j, col, 0.0)
            return carry

        lax.fori_loop(0, GLA_BLOCK, column, 0)
        b_edge = b[edge_row:edge_row + 1, :]
        s_decay = jnp.exp(b_edge)
        q_dec = (q * jnp.exp(b)).astype(BF16)
        kd_t = (k * jnp.exp(b_edge - b)).astype(BF16).T
        dec_t = _decay_columns(s_decay)
        v = v_ref[rows, :]
        for h in range(GLA_HEADS):
            s = s_ref[h]
            a = jnp.where(mask, sc_scr[h], 0.0).astype(BF16)
            lhs = jnp.concatenate([q_dec[:, ks[h]], a], axis=1)
            rhs = jnp.concatenate([s.astype(BF16), v[:, vs[h]]], axis=0)
            o_ref[rows, vs[h]] = _dot(lhs, rhs).astype(o_ref.dtype)
            s_ref[h] = _scale_state(s, dec_t[ks[h], :]) + _dot(kd_t[ks[h], :], v[:, vs[h]])
        return carry

    lax.fori_loop(0, n_blocks, block, 0)


def _gla_kernel(gmin_ref, qf_ref, kf_ref, vf_ref, gf_ref, qb_ref, kb_ref, vb_ref, gb_ref,
                of_ref, ob_ref, sf_ref, sb_ref, b_scr, q_scr, k_scr, sc_scr, *, n_blocks):
    bi = pl.program_id(0)
    t = pl.program_id(1)
    nt = pl.num_programs(1)

    @pl.when(t == 0)
    def _():
        sf_ref[...] = jnp.zeros_like(sf_ref)
        sb_ref[...] = jnp.zeros_like(sb_ref)

    ri = lax.broadcasted_iota(jnp.int32, (GLA_BLOCK, GLA_BLOCK), 0)
    ci = lax.broadcasted_iota(jnp.int32, (GLA_BLOCK, GLA_BLOCK), 1)
    masks = (ri >= ci, ri <= ci)
    cums = (masks[0].astype(BF16), masks[1].astype(BF16))
    fwd_refs = (qf_ref, kf_ref, vf_ref, gf_ref)
    bwd_refs = (qb_ref, kb_ref, vb_ref, gb_ref)

    g_min = jnp.minimum(gmin_ref[bi * nt + t], gmin_ref[bi * nt + (nt - 1 - t)])
    factorisable = g_min * GLA_BLOCK > -EXP_RANGE

    @pl.when(factorisable)
    def _():
        offs = [c * GLA_BLOCK for c in range(n_blocks)]
        _gla_fast(fwd_refs, bwd_refs, of_ref, ob_ref, sf_ref, sb_ref, cums, masks, offs, offs[::-1])

    @pl.when(jnp.logical_not(factorisable))
    def _():
        scr = (b_scr, q_scr, k_scr, sc_scr)
        _gla_exact(fwd_refs, of_ref, sf_ref, cums[0], masks[0], GLA_BLOCK - 1, False, n_blocks, *scr)
        _gla_exact(bwd_refs, ob_ref, sb_ref, cums[1], masks[1], 0, True, n_blocks, *scr)


def _gla(q, k, v, g, g_min, batch, seq, tile):
    n = batch * seq
    nt = seq // tile
    fwd = lambda b, t, gm: (b * nt + t, 0)
    bwd = lambda b, t, gm: (b * nt + (nt - 1 - t), 0)
    bwd_g = lambda b, t, gm: (b * nt + (nt - 1 - t), 1)
    in_specs = [
        pl.BlockSpec((tile, QK_W), fwd), pl.BlockSpec((tile, QK_W), fwd),
        pl.BlockSpec((tile, V_W), fwd), pl.BlockSpec((tile, QK_W), fwd),
        pl.BlockSpec((tile, QK_W), bwd), pl.BlockSpec((tile, QK_W), bwd),
        pl.BlockSpec((tile, V_W), bwd), pl.BlockSpec((tile, QK_W), bwd_g),
    ]
    state = pltpu.VMEM((GLA_HEADS, GLA_DK, GLA_DV), F32)
    rows_f32 = pltpu.VMEM((GLA_BLOCK, QK_W), F32)
    scratch = [state, state, rows_f32, rows_f32, rows_f32, pltpu.VMEM((GLA_HEADS, GLA_BLOCK, GLA_BLOCK), F32)]
    block_bytes = 2 * (2 * _nbytes((tile, QK_W), BF16) + 2 * _nbytes((tile, V_W), BF16)
                       + _nbytes((tile, QK_W), BF16))
    scratch_bytes = (2 * _nbytes((GLA_HEADS, GLA_DV, GLA_DK), F32) + 3 * _nbytes((GLA_BLOCK, QK_W), F32)
                     + _nbytes((GLA_HEADS, GLA_BLOCK, GLA_BLOCK), F32))
    grid_spec = pltpu.PrefetchScalarGridSpec(
        num_scalar_prefetch=1,
        grid=(batch, nt),
        in_specs=in_specs,
        out_specs=[pl.BlockSpec((tile, V_W), fwd), pl.BlockSpec((tile, V_W), bwd)],
        scratch_shapes=scratch,
    )
    return pl.pallas_call(
        functools.partial(_gla_kernel, n_blocks=tile // GLA_BLOCK),
        grid_spec=grid_spec,
        out_shape=[jax.ShapeDtypeStruct((n, V_W), BF16)] * 2,
        compiler_params=pltpu.CompilerParams(
            dimension_semantics=("arbitrary", "arbitrary"),
            vmem_limit_bytes=_vmem_limit(block_bytes, scratch_bytes)),
        name="gla",
    )(g_min, q, k, v, g, q, k, v, g)


def _mix_out_kernel(of_ref, ob_ref, sr_ref, u_ref, up_ref, un_ref, sga_ref, sgb_ref, x_ref,
                    og_ref, wgrp_ref, ps_ref, wao_ref, wb_ref, y_ref, *, tile, seq):
    t = pl.program_id(1)
    nt = pl.num_programs(1)
    sub = SUB_ROWS
    n_sub = tile // sub

    u_prev = jnp.where(t > 0, up_ref[...].astype(F32), 0.0)
    u_next = jnp.where(t < nt - 1, un_ref[...].astype(F32), 0.0)
    u_ext = jnp.concatenate([u_prev, u_ref[...].astype(F32), u_next], axis=0)

    def window_mean(centred, w, ahead, r0):
        def edge(e0):
            pos = t * tile + r0 + e0 + lax.broadcasted_iota(jnp.int32, (HALO, POOL_GC), 0)
            cnt = jnp.minimum(pos + ahead, seq - 1) - jnp.maximum(pos - w // 2, 0) + 1
            return centred[e0:e0 + HALO, :] / cnt.astype(F32)
        inner = centred[HALO:sub - HALO, :] * (1.0 / w)
        return jnp.concatenate([edge(0), inner, edge(sub - HALO)], axis=0)

    def branches(i):
        r0 = i * sub
        rows = slice(r0, r0 + sub)
        o = of_ref[rows, :].astype(F32) + ob_ref[rows, :].astype(F32)
        normed = []
        for h in range(GLA_HEADS):
            oh = o[:, h * GLA_DV:(h + 1) * GLA_DV]
            normed.append(oh * lax.rsqrt(jnp.mean(oh * oh, axis=-1, keepdims=True) + EPS))
        a_in = (jnp.concatenate(normed, axis=-1) * og_ref[...]).astype(BF16) * sr_ref[rows, :]
        a_out = _dot(a_in, wao_ref[:, :D_MODEL])
        ext = sub + 2 * HALO
        pooled = []
        for gi, w in enumerate(POOL_WINDOWS):
            ug = u_ext[r0:r0 + ext, gi * POOL_GC:(gi + 1) * POOL_GC]
            trail = ug
            span = 1
            while span < w:
                trail = trail + pltpu.roll(trail, span, 0)
                span *= 2
            ahead = w - 1 - w // 2
            centred = pltpu.roll(trail, ext - ahead, 0) if ahead else trail
            p = window_mean(centred[HALO:HALO + sub, :], w, ahead, r0) - ug[HALO:HALO + sub, :]
            pooled.append(_dot(p.astype(BF16), wgrp_ref[gi]))
        pb = (jnp.concatenate(pooled, axis=-1) * ps_ref[...]).astype(BF16)
        b_out = _dot(pb, wb_ref[:, :D_MODEL])
        return a_out, b_out

    def merge(i, a_out, b_out):
        rows = slice(i * sub, (i + 1) * sub)
        merged = sga_ref[rows, :] * a_out.astype(BF16) + sgb_ref[rows, :] * b_out.astype(BF16)
        y_ref[rows, :] = x_ref[rows, :] + _dot(merged, wao_ref[:, D_MODEL:2 * D_MODEL])

    pending = branches(0)
    for i in range(1, n_sub):
        nxt = branches(i)
        merge(i - 1, *pending)
        pending = nxt
    merge(n_sub - 1, *pending)


def _mix_out(o_f, o_b, sr, u, sga, sgb, x, og, wgrp, ps, wao, wb, l, batch, seq, tile):
    n = batch * seq
    nt = seq // tile
    hb = tile // HALO
    n_hb = n // HALO
    row = lambda b, t: (b * nt + t, 0)
    prev = lambda b, t: (jnp.maximum((b * nt + t) * hb - 1, 0), 0)
    nxt = lambda b, t: (jnp.minimum((b * nt + t + 1) * hb, n_hb - 1), 0)
    in_specs = [
        pl.BlockSpec((tile, V_W), row), pl.BlockSpec((tile, V_W), row), pl.BlockSpec((tile, V_W), row),
        pl.BlockSpec((tile, POOL_WIDTH), row), pl.BlockSpec((HALO, POOL_WIDTH), prev),
        pl.BlockSpec((HALO, POOL_WIDTH), nxt),
        pl.BlockSpec((tile, D_MODEL), row), pl.BlockSpec((tile, D_MODEL), row), pl.BlockSpec((tile, D_MODEL), row),
        _layer_spec((1, V_W), l), _layer_spec((POOL_GROUPS, POOL_GC, POOL_GC), l),
        _layer_spec((1, POOL_WIDTH), l),
        _layer_spec((V_W, 2 * D_MODEL + 2 * ODD_PITCH_PAD), l), _layer_spec((POOL_WIDTH, D_MODEL + ODD_PITCH_PAD), l),
    ]
    block_bytes = (5 * _nbytes((tile, D_MODEL), BF16) + _nbytes((tile + 2 * HALO, POOL_WIDTH), BF16)
                   + 2 * _nbytes((tile, D_MODEL), F32))
    weight_bytes = (_nbytes((V_W + POOL_WIDTH + D_MODEL, D_MODEL), BF16)
                    + _nbytes((POOL_GROUPS, POOL_GC, POOL_GC), BF16))
    return pl.pallas_call(
        functools.partial(_mix_out_kernel, tile=tile, seq=seq),
        grid=(batch, nt),
        in_specs=in_specs,
        out_specs=pl.BlockSpec((tile, D_MODEL), row),
        out_shape=jax.ShapeDtypeStruct((n, D_MODEL), F32),
        compiler_params=pltpu.CompilerParams(
            dimension_semantics=("arbitrary", "arbitrary"),
            vmem_limit_bytes=_vmem_limit(block_bytes, weight_bytes)),
        name="mix_out",
    )(o_f, o_b, sr, u, u, u, sga, sgb, x, og, wgrp, ps, wao, wb)


def _conv_ffn_kernel(x_ref, xp_ref, xn_ref, g2_ref, wup_ref, cw_ref, cb_ref, wdn_ref, gf_ref,
                     y_ref, xe_ref, hid_ref, *, tile, final_norm):
    t = pl.program_id(1)
    nt = pl.num_programs(1)
    g2 = g2_ref[...]
    x = x_ref[...]
    xe_ref[0:HALO, :] = jnp.where(t > 0, _rmsnorm(xp_ref[...], g2), 0.0).astype(BF16)
    xe_ref[HALO:HALO + tile, :] = _rmsnorm(x, g2).astype(BF16)
    xe_ref[HALO + tile:, :] = jnp.where(t < nt - 1, _rmsnorm(xn_ref[...], g2), 0.0).astype(BF16)
    ext = tile + 2 * HALO
    n_chunks = D_FF // FF_CHUNK

    def up(j):
        a = _dot(xe_ref[...], wup_ref[:, j * FF_CHUNK:(j + 1) * FF_CHUNK])
        val = _dot(xe_ref[HALO:HALO + tile, :], wup_ref[:, D_FF + j * FF_CHUNK:D_FF + (j + 1) * FF_CHUNK])
        return a, val

    def act(j, a, val):
        cs = slice(j * FF_CHUNK, (j + 1) * FF_CHUNK)
        cw = cw_ref[:, cs]
        conv = (pltpu.roll(a, 1, 0) * cw[0:1, :] + a * cw[1:2, :]
                + pltpu.roll(a, ext - 1, 0) * cw[2:3, :])[HALO:HALO + tile, :] + cb_ref[:, cs]
        hid_ref[:, cs] = (conv * jax.nn.sigmoid(conv) * val).astype(BF16)

    split = (n_chunks - 2) * FF_CHUNK
    nxt = up(0)
    down = None
    for j in range(n_chunks):
        cur = nxt
        if j + 1 < n_chunks:
            nxt = up(j + 1)
        if j == n_chunks - 2:
            down = _dot(hid_ref[:, :split], wdn_ref[:split, :D_MODEL])
        act(j, *cur)
    y = x + (down + _dot(hid_ref[:, split:], wdn_ref[split:, :D_MODEL]))
    if final_norm:
        y = _rmsnorm(y, gf_ref[...])
    y_ref[...] = y


def _conv_ffn(x, g2, wup, cw, cb, wdn, gf, l, batch, seq, tile, final_norm):
    n = batch * seq
    nt = seq // tile
    hb = tile // HALO
    n_hb = n // HALO
    row = lambda b, t: (b * nt + t, 0)
    prev = lambda b, t: (jnp.maximum((b * nt + t) * hb - 1, 0), 0)
    nxt = lambda b, t: (jnp.minimum((b * nt + t + 1) * hb, n_hb - 1), 0)
    in_specs = [
        pl.BlockSpec((tile, D_MODEL), row), pl.BlockSpec((HALO, D_MODEL), prev), pl.BlockSpec((HALO, D_MODEL), nxt),
        _layer_spec((1, D_MODEL), l),
        _layer_spec((D_MODEL, 2 * D_FF), l),
        _layer_spec((3, D_FF), l), _layer_spec((1, D_FF), l),
        _layer_spec((D_FF, D_MODEL + ODD_PITCH_PAD), l),
        _layer_spec((1, D_MODEL), 0),
    ]
    scratch = [pltpu.VMEM((tile + 2 * HALO, D_MODEL), BF16), pltpu.VMEM((tile, D_FF), BF16)]
    block_bytes = 2 * _nbytes((tile + HALO, D_MODEL), F32)
    scratch_bytes = (_nbytes((D_MODEL, 2 * D_FF), BF16) + _nbytes((D_FF, D_MODEL), BF16)
                     + _nbytes((tile + 2 * HALO, D_MODEL), BF16) + _nbytes((tile, D_FF), BF16))
    return pl.pallas_call(
        functools.partial(_conv_ffn_kernel, tile=tile, final_norm=final_norm),
        grid=(batch, nt),
        in_specs=in_specs,
        out_specs=pl.BlockSpec((tile, D_MODEL), row),
        out_shape=jax.ShapeDtypeStruct((n, D_MODEL), F32),
        scratch_shapes=scratch,
        compiler_params=pltpu.CompilerParams(
            dimension_semantics=("arbitrary", "arbitrary"),
            vmem_limit_bytes=_vmem_limit(block_bytes, scratch_bytes)),
        name="conv_ffn",
    )(x, x, x, g2, wup, cw, cb, wdn, gf)


def _tiles(seq):
    small, big = min(1024, seq), min(1024, seq)
    for tok in (small, big):
        assert seq % tok == 0 and tok % SUB_ROWS == 0 and tok % GLA_BLOCK == 0
    return small, big


def _prep_weights(norm1_g, w_in, w_lr2_f, b_lr_f, w_lr2_b, b_lr_b, onorm_g, w_pool_grp, pool_scale, w_br_a,
                  w_br_b, w_out, norm2_g, w_up, conv_w, conv_b, w_down, final_g):
    lr0 = 2 * QK_W + 2 * V_W
    lr1 = lr0 + 2 * GLA_LOWRANK
    depth = w_in.shape[0]
    wlr2 = jnp.zeros((depth, LR_PAD, 2 * QK_W + ODD_PITCH_PAD), F32)
    wlr2 = wlr2.at[:, :GLA_LOWRANK, :QK_W].set(w_lr2_f)
    wlr2 = wlr2.at[:, GLA_LOWRANK:2 * GLA_LOWRANK, QK_W:2 * QK_W].set(w_lr2_b)
    row = lambda a: a[:, None, :]
    wide = lambda a, n: jnp.pad(a, ((0, 0), (0, 0), (0, n * ODD_PITCH_PAD))).astype(BF16)
    return dict(
        wm=jnp.concatenate([w_in[:, :, :lr0], w_in[:, :, lr1:]], axis=2).astype(BF16),
        wlr=jnp.pad(w_in[:, :, lr0:lr1], ((0, 0), (0, 0), (0, LR_PAD - 2 * GLA_LOWRANK))).astype(BF16),
        wlr2=wlr2.astype(BF16), blr=row(jnp.concatenate([b_lr_f, b_lr_b], axis=1)),
        wgrp=w_pool_grp.astype(BF16), wao=wide(jnp.concatenate([w_br_a, w_out], axis=2), 2), wb=wide(w_br_b, 1),
        wup=w_up.astype(BF16), wdn=wide(w_down, 1),
        g1=row(norm1_g), og=row(onorm_g), ps=row(pool_scale), g2=row(norm2_g), cw=conv_w, cb=row(conv_b),
        gf=final_g[None, None, :])


def _trunk(x, w):
    batch, seq, _ = x.shape
    tile_in, tile = _tiles(seq)
    depth = w["wm"].shape[0]
    x = x.reshape(batch * seq, D_MODEL)
    for l in range(depth):
        q, k, v, sr, u, sga, sgb, g, g_min = _in_proj(x, w["g1"], w["wm"], w["wlr"], w["wlr2"], w["blr"], l, tile_in)
        g_min = jnp.min(g_min[:, 0, 0].reshape(-1, tile // tile_in), axis=1)
        o_f, o_b = _gla(q, k, v, g, g_min, batch, seq, tile)
        x = _mix_out(o_f, o_b, sr, u, sga, sgb, x, w["og"], w["wgrp"], w["ps"], w["wao"], w["wb"],
                     l, batch, seq, tile)
        x = _conv_ffn(x, w["g2"], w["wup"], w["cw"], w["cb"], w["wdn"], w["gf"], l, batch, seq, tile,
                      final_norm=(l == depth - 1))
    return x.reshape(batch, seq, D_MODEL)


def kernel(x_prompt, x_sample, norm1_g, w_in, w_lr2_f, b_lr_f, w_lr2_b, b_lr_b, onorm_g, w_pool_grp, pool_scale, w_br_a, w_br_b, w_out, norm2_g, w_up, conv_w, conv_b, w_down, final_g):
    w = _prep_weights(norm1_g, w_in, w_lr2_f, b_lr_f, w_lr2_b, b_lr_b, onorm_g, w_pool_grp, pool_scale, w_br_a, w_br_b,
                      w_out, norm2_g, w_up, conv_w, conv_b, w_down, final_g)
    return (_trunk(x_prompt, w), _trunk(x_sample, w))
```
